```python
import jax, jax.numpy as jnp
from jax import lax
import numpy as np

D_MODEL = 1024
BATCH = 32
SEQ = 2048
DEPTH = 1
DEC_BATCH = 2
DEC_SEQ = 16384
PAST_LEN = 128

POOL_GROUPS = 4
POOL_GROUP_DIM = 128
POOL_WIDTH = POOL_GROUPS * POOL_GROUP_DIM
POOL_WINDOWS = (2, 4, 8, 16)
HEAD_DIM = 64
N_HEADS = D_MODEL // HEAD_DIM
RWKV_WIDTH = N_HEADS * HEAD_DIM
DECAY_LORA = 64
AAA_LORA = 64
GATE_LORA = 128
D_FF = 4 * D_MODEL
N_BRANCHES = 2
RMS_EPS = 1e-6
GN_EPS = 64e-5
L2_EPS = 1e-12

COL_POOL = 0
COL_R = COL_POOL + POOL_WIDTH
COL_K = COL_R + RWKV_WIDTH
COL_V = COL_K + RWKV_WIDTH
COL_W = COL_V + RWKV_WIDTH
COL_A = COL_W + DECAY_LORA
COL_G = COL_A + AAA_LORA
COL_GATE = COL_G + GATE_LORA
IN_COLS = COL_GATE + N_BRANCHES * D_MODEL
SHIFT_WIDTH = COL_GATE - COL_R

kernel_name = "pool_rwkv7_bidir_hybrid_encoder"


def rms_norm(x, g):
    xf = x.astype(jnp.float32)
    y = xf * lax.rsqrt(jnp.mean(xf * xf, axis=-1, keepdims=True) + RMS_EPS)
    return (y * g.astype(jnp.float32)).astype(x.dtype)


def centred_shift_mix(z, mu_prev, mu_next):
    z_prev = jnp.pad(z, ((0, 0), (1, 0), (0, 0)))[:, :-1]
    z_next = jnp.pad(z, ((0, 0), (0, 1), (0, 0)))[:, 1:]
    return z + mu_prev * (z_prev - z) + mu_next * (z_next - z)


def multiscale_pool(u):
    B, S, _ = u.shape
    ug = u.reshape(B, S, POOL_GROUPS, POOL_GROUP_DIM).astype(jnp.float32)
    cs = jnp.concatenate([jnp.zeros((B, 1, POOL_GROUPS, POOL_GROUP_DIM), jnp.float32),
                          jnp.cumsum(ug, axis=1)], axis=1)
    t = np.arange(S)
    outs = []
    for gi, w in enumerate(POOL_WINDOWS):
        lo = np.maximum(t - w // 2, 0)
        hi = np.minimum(t + w // 2 - 1, S - 1)
        cnt = (hi - lo + 1).astype(np.float32)[None, :, None]
        csg = cs[:, :, gi]
        outs.append((csg[:, hi + 1] - csg[:, lo]) / cnt - ug[:, :, gi])
    return jnp.stack(outs, axis=2)


def wkv7_scan(r, k, v, w, kk, a, reverse):
    B, _, H, N = r.shape
    xs = tuple(jnp.moveaxis(t_, 1, 0) for t_ in (r, k, v, w, -kk, kk * a))

    def step(state, inp):
        r_t, k_t, v_t, w_t, a_t, b_t = inp
        sa = jnp.einsum('bhvk,bhk->bhv', state, a_t)
        state = (state * w_t[:, :, None, :] + sa[..., None] * b_t[:, :, None, :]
                 + v_t[..., None] * k_t[:, :, None, :])
        return state, jnp.einsum('bhvk,bhk->bhv', state, r_t)

    s0 = jnp.zeros((B, H, N, N), jnp.float32)
    _, ys = lax.scan(step, s0, xs, reverse=reverse)
    return jnp.moveaxis(ys, 0, 1)


def head_group_norm(y, w, b):
    mean = jnp.mean(y, axis=-1, keepdims=True)
    var = jnp.mean(jnp.square(y - mean), axis=-1, keepdims=True)
    return ((y - mean) * lax.rsqrt(var + GN_EPS) * w.reshape(N_HEADS, HEAD_DIM).astype(jnp.float32)
            + b.reshape(N_HEADS, HEAD_DIM).astype(jnp.float32))


def rwkv_branch(zr, zk, zv, zw, za, zg, k_k, k_a, r_k, w0_f, w_up_f, a0_f, a_up_f,
                w0_b, w_up_b, a0_b, a_up_b, g_up, ln_w, ln_b):
    B, S, _ = zr.shape
    heads = lambda t_: t_.astype(jnp.float32).reshape(B, S, N_HEADS, HEAD_DIM)
    r, k, v = heads(zr), heads(zk), heads(zv)
    kk = heads(zk * k_k)
    kk = kk / jnp.maximum(jnp.linalg.norm(kk, axis=-1, keepdims=True), L2_EPS)
    k_a_h = k_a.reshape(N_HEADS, HEAD_DIM).astype(jnp.float32)
    tw = jnp.tanh(zw)
    y_sum = jnp.zeros_like(r)
    k_sum = jnp.zeros_like(r)
    for w0, w_up, a0, a_up, rev in ((w0_f, w_up_f, a0_f, a_up_f, False),
                                    (w0_b, w_up_b, a0_b, a_up_b, True)):
        w_raw = (w0 + tw @ w_up).astype(jnp.float32)
        w = jnp.exp(-jnp.exp(-jax.nn.softplus(-w_raw) - 0.5))
        a = heads(jax.nn.sigmoid(a0 + za @ a_up))
        k_d = k * (1.0 + (a - 1.0) * k_a_h)
        y_sum = y_sum + wkv7_scan(r, k_d, v, heads(w), kk, a, rev)
        k_sum = k_sum + k_d
    y = head_group_norm(y_sum, ln_w, ln_b)
    y = y + jnp.sum(r * k_sum * r_k.astype(jnp.float32), axis=-1, keepdims=True) * v
    g = jax.nn.sigmoid(zg) @ g_up
    return (y.reshape(B, S, RWKV_WIDTH) * g.astype(jnp.float32)).astype(zr.dtype)


def encoder_layer(x, g_mix, w_in, b_gate, mu_prev, mu_next, pool_w, pool_scale, w_pool_br,
                  k_k, k_a, r_k, w0_f, w_up_f, a0_f, a_up_f, w0_b, w_up_b, a0_b, a_up_b,
                  g_up, ln_w, ln_b, w_rwkv_br, w_out, g_ffn, w_ff1, w_ff2):
    B, S, _ = x.shape
    xn = rms_norm(x, g_mix)
    z = xn @ w_in
    pooled = multiscale_pool(z[..., COL_POOL:COL_R])
    pooled = jnp.einsum('bsgc,gcd->bsgd', pooled, pool_w.astype(jnp.float32))
    pooled = (pooled.reshape(B, S, POOL_WIDTH) * pool_scale.astype(jnp.float32)).astype(x.dtype)
    pool_out = pooled @ w_pool_br
    zs = centred_shift_mix(z[..., COL_R:COL_GATE], mu_prev, mu_next)
    o = lambda c0, c1: zs[..., c0 - COL_R:c1 - COL_R]
    rw = rwkv_branch(o(COL_R, COL_K), o(COL_K, COL_V), o(COL_V, COL_W), o(COL_W, COL_A),
                     o(COL_A, COL_G), o(COL_G, COL_GATE), k_k, k_a, r_k, w0_f, w_up_f, a0_f, a_up_f,
                     w0_b, w_up_b, a0_b, a_up_b, g_up, ln_w, ln_b)
    rwkv_out = rw @ w_rwkv_br
    gates = jax.nn.sigmoid(z[..., COL_GATE:] + b_gate)
    merged = gates[..., :D_MODEL] * pool_out + gates[..., D_MODEL:] * rwkv_out
    x = x + merged @ w_out
    hn = rms_norm(x, g_ffn)
    return x + jnp.square(jax.nn.relu(hn @ w_ff1)) @ w_ff2


def setup_inputs(seed: int = 0) -> dict:
    key = jax.random.key(seed)
    ks = iter(jax.random.split(key, 40))
    nrm = lambda shape, scale: scale * jax.random.normal(next(ks), shape, jnp.float32)
    uni = lambda shape, lo, hi: jax.random.uniform(next(ks), shape, jnp.float32, lo, hi)
    L = DEPTH
    return {
        "x_prompt": nrm((BATCH, SEQ, D_MODEL), 1.0),
        "x_sample": nrm((DEC_BATCH, DEC_SEQ, D_MODEL), 1.0),
        "g_mix": 1.0 + nrm((L, D_MODEL), 0.05),
        "w_in": nrm((L, D_MODEL, IN_COLS), D_MODEL ** -0.5),
        "b_gate": nrm((L, N_BRANCHES * D_MODEL), 0.1),
        "mu_prev": uni((L, SHIFT_WIDTH), 0.1, 0.5),
        "mu_next": uni((L, SHIFT_WIDTH), 0.1, 0.5),
        "pool_w": nrm((L, POOL_GROUPS, POOL_GROUP_DIM, POOL_GROUP_DIM), POOL_GROUP_DIM ** -0.5),
        "pool_scale": 1.0 + nrm((L, POOL_WIDTH), 0.1),
        "w_pool_br": nrm((L, POOL_WIDTH, D_MODEL), POOL_WIDTH ** -0.5),
        "k_k": 0.85 + nrm((L, RWKV_WIDTH), 0.05),
        "k_a": 1.0 + nrm((L, RWKV_WIDTH), 0.05),
        "r_k": nrm((L, N_HEADS, HEAD_DIM), 0.1),
        "w0_f": uni((L, RWKV_WIDTH), -5.0, 1.0),
        "w_up_f": nrm((L, DECAY_LORA, RWKV_WIDTH), 0.5 * DECAY_LORA ** -0.5),
        "a0_f": nrm((L, RWKV_WIDTH), 0.5),
        "a_up_f": nrm((L, AAA_LORA, RWKV_WIDTH), 0.5 * AAA_LORA ** -0.5),
        "w0_b": uni((L, RWKV_WIDTH), -5.0, 1.0),
        "w_up_b": nrm((L, DECAY_LORA, RWKV_WIDTH), 0.5 * DECAY_LORA ** -0.5),
        "a0_b": nrm((L, RWKV_WIDTH), 0.5),
        "a_up_b": nrm((L, AAA_LORA, RWKV_WIDTH), 0.5 * AAA_LORA ** -0.5),
        "g_up": nrm((L, GATE_LORA, RWKV_WIDTH), GATE_LORA ** -0.5),
        "ln_w": 1.0 + nrm((L, RWKV_WIDTH), 0.05),
        "ln_b": nrm((L, RWKV_WIDTH), 0.02),
        "w_rwkv_br": nrm((L, RWKV_WIDTH, D_MODEL), RWKV_WIDTH ** -0.5),
        "w_out": nrm((L, D_MODEL, D_MODEL), D_MODEL ** -0.5),
        "g_ffn": 1.0 + nrm((L, D_MODEL), 0.05),
        "w_ff1": nrm((L, D_MODEL, D_FF), D_MODEL ** -0.5),
        "w_ff2": nrm((L, D_FF, D_MODEL), D_FF ** -0.5),
        "g_final": 1.0 + nrm((D_MODEL,), 0.05),
    }


def reference(x_prompt, x_sample, g_mix, w_in, b_gate, mu_prev, mu_next, pool_w, pool_scale,
              w_pool_br, k_k, k_a, r_k, w0_f, w_up_f, a0_f, a_up_f, w0_b, w_up_b, a0_b, a_up_b,
              g_up, ln_w, ln_b, w_rwkv_br, w_out, g_ffn, w_ff1, w_ff2, g_final):
    def trunk(x):
        for l in range(DEPTH):
            x = encoder_layer(x, g_mix[l], w_in[l], b_gate[l], mu_prev[l], mu_next[l], pool_w[l],
                              pool_scale[l], w_pool_br[l], k_k[l], k_a[l], r_k[l], w0_f[l], w_up_f[l],
                              a0_f[l], a_up_f[l], w0_b[l], w_up_b[l], a0_b[l], a_up_b[l], g_up[l],
                              ln_w[l], ln_b[l], w_rwkv_br[l], w_out[l], g_ffn[l], w_ff1[l], w_ff2[l])
        return rms_norm(x, g_final)

    y_prompt = trunk(x_prompt)
    y_sample = trunk(x_sample)
    return (y_prompt, y_sample)
```

```python
import functools
import math

import jax
import jax.numpy as jnp
import numpy as np
from jax import lax
from jax.experimental import pallas as pl
from jax.experimental.pallas import tpu as pltpu

D = 1024
HEAD = 64
N_HEADS = D // HEAD
POOL_WIDTH = 512
POOL_GROUP = 128
POOL_WINDOWS = (2, 4, 8, 16)
POOL_HALO = 8
RW_COLS = 3 * D + 256
GATE_COLS = 2 * D
D_FF = 4 * D
RMS_EPS = 1e-6
GN_EPS = 64e-5
L2_EPS = 1e-12
CHUNK = 64
SEG = 256
VMEM_LIMIT = 56 * 1024 * 1024

F32 = jnp.float32
BF16 = jnp.bfloat16


def _dot(a, b):
    return jnp.dot(a, b, preferred_element_type=F32)


def _dot_nt(a, b):
    return lax.dot_general(a, b, (((1,), (1,)), ((), ())), preferred_element_type=F32)


def _dot_tn(a, b):
    return lax.dot_general(a, b, (((0,), (0,)), ((), ())), preferred_element_type=F32)


def _split(x):
    hi = x.astype(BF16)
    lo = (x - hi.astype(F32)).astype(BF16)
    return hi, lo


def _seg_sum(x, bd):
    outs = []
    for g in range(D // SEG):
        hi, lo = _split(x[:, g * SEG:(g + 1) * SEG])
        outs.append(_dot(hi, bd) + _dot(lo, bd))
    return jnp.concatenate(outs, axis=1)


def _rms(x, g):
    return x * lax.rsqrt(jnp.mean(x * x, axis=-1, keepdims=True) + RMS_EPS) * g


def _sigmoid(x):
    return 1.0 / (1.0 + jnp.exp(-x))


def _cparams(n_axes, arbitrary=False):
    sem = ("arbitrary",) * n_axes if arbitrary else ("parallel",) * n_axes
    return pltpu.CompilerParams(dimension_semantics=sem, vmem_limit_bytes=VMEM_LIMIT)


def _const_spec(shape):
    nd = len(shape)
    return pl.BlockSpec(shape, lambda *_: (0,) * nd, pipeline_mode=pl.Buffered(1))


def _in_proj_kernel(x_ref, g_ref, w_ref, zp_ref, zr_ref, zg_ref):
    xn = _rms(x_ref[...], g_ref[...]).astype(BF16)
    zp_ref[...] = _dot(xn, w_ref[:, 0:POOL_WIDTH])
    zr_ref[...] = _dot(xn, w_ref[:, POOL_WIDTH:POOL_WIDTH + RW_COLS])
    zg_ref[...] = _dot(xn, w_ref[:, POOL_WIDTH + RW_COLS:])


def _in_proj(x, g_mix, w_in, tm):
    T = x.shape[0]
    row = lambda i: (i, 0)
    return pl.pallas_call(
        _in_proj_kernel,
        grid=(T // tm,),
        in_specs=[pl.BlockSpec((tm, D), row), _const_spec((1, D)), _const_spec(w_in.shape)],
        out_specs=[pl.BlockSpec((tm, POOL_WIDTH), row), pl.BlockSpec((tm, RW_COLS), row),
                   pl.BlockSpec((tm, GATE_COLS), row)],
        out_shape=[jax.ShapeDtypeStruct((T, POOL_WIDTH), F32), jax.ShapeDtypeStruct((T, RW_COLS), F32),
                   jax.ShapeDtypeStruct((T, GATE_COLS), F32)],
        compiler_params=_cparams(1),
        name="in_proj",
    )(x, g_mix, w_in)


def _prep_kernel(seq, tm, z_ref, zprev_ref, znext_ref, mup_ref, mun_ref, kk_ref, ka_ref, rk_ref,
                 w0_ref, a0_ref, wup_ref, aup_ref, bd_ref, trif_ref, trib_ref,
                 at_ref, bt_ref, kt_ref, rt_ref, v_ref, wc_ref, bonus_ref, sg_ref):
    pos0 = (pl.program_id(0) * tm) % seq
    at_start = pos0 == 0
    at_end = pos0 + tm == seq
    rows = lax.broadcasted_iota(jnp.int32, (tm, 1), 0)

    def mix(c0, c1):
        z = z_ref[:, c0:c1]
        prev_row = jnp.where(at_start, 0.0, zprev_ref[POOL_HALO - 1:POOL_HALO, c0:c1])
        next_row = jnp.where(at_end, 0.0, znext_ref[0:1, c0:c1])
        z_prev = jnp.where(rows == 0, prev_row, pltpu.roll(z, 1, axis=0))
        z_next = jnp.where(rows == tm - 1, next_row, pltpu.roll(z, tm - 1, axis=0))
        return z + mup_ref[:, c0:c1] * (z_prev - z) + mun_ref[:, c0:c1] * (z_next - z)

    r = mix(0, D)
    k = mix(D, 2 * D)
    v = mix(2 * D, 3 * D)
    zwa = mix(3 * D, 3 * D + 128)
    zg = mix(3 * D + 128, 3 * D + 256)
    sg_ref[...] = _sigmoid(zg)
    v_ref[...] = v.astype(BF16)

    w_raw = w0_ref[...] + _dot(jnp.tanh(zwa).astype(BF16), wup_ref[...])
    a_all = _sigmoid(a0_ref[...] + _dot(zwa.astype(BF16), aup_ref[...]))
    lw_all = -_sigmoid(w_raw) * math.exp(-0.5)

    bd = bd_ref[...]
    kkr = k * kk_ref[...]
    ss = _seg_sum(kkr * kkr, bd)
    kk = kkr * jnp.minimum(lax.rsqrt(ss), 1.0 / L2_EPS)

    k_sum = jnp.zeros_like(k)
    for d, tri_ref in enumerate((trif_ref, trib_ref)):
        a = a_all[:, d * D:(d + 1) * D]
        lw = lw_all[:, d * D:(d + 1) * D]
        kd = k * (1.0 + (a - 1.0) * ka_ref[...])
        k_sum = k_sum + kd
        hi, lo = _split(lw)
        tri = tri_ref[...]
        cum = _dot(tri, hi) + _dot(tri, lo)
        e_neg = jnp.exp(-cum)
        rt_ref[d] = (r * jnp.exp(cum)).astype(BF16)
        bt_ref[d] = (kk * a * e_neg).astype(BF16)
        kt_ref[d] = (kd * e_neg).astype(BF16)
        at_ref[d] = (-kk * jnp.exp(cum - lw)).astype(BF16)
        for c in range(tm // CHUNK):
            last = c * CHUNK + (CHUNK - 1 if d == 0 else 0)
            wc_ref[d, c] = jnp.exp(cum[last:last + 1, :])
    bonus_ref[...] = _seg_sum(r * k_sum * rk_ref[...], bd) * v


def _prep(z_rw, seq, tm, p):
    T = z_rw.shape[0]
    nb = tm // POOL_HALO
    last_blk = T // POOL_HALO - 1
    row = lambda i: (i, 0)
    drow = lambda i: (0, i, 0)
    consts = [p["mu_prev"], p["mu_next"], p["k_k"], p["k_a"], p["r_k"], p["w0"], p["a0"], p["w_up"],
              p["a_up"], p["bd"], p["tri_f"], p["tri_b"]]
    tok = lambda dt: jax.ShapeDtypeStruct((2, T, D), dt)
    return pl.pallas_call(
        functools.partial(_prep_kernel, seq, tm),
        grid=(T // tm,),
        in_specs=[pl.BlockSpec((tm, RW_COLS), row),
                  pl.BlockSpec((POOL_HALO, RW_COLS), lambda i: (jnp.maximum(i * nb - 1, 0), 0)),
                  pl.BlockSpec((POOL_HALO, RW_COLS), lambda i: (jnp.minimum((i + 1) * nb, last_blk), 0))]
                 + [_const_spec(c.shape) for c in consts],
        out_specs=[pl.BlockSpec((2, tm, D), drow)] * 4
                  + [pl.BlockSpec((tm, D), row),
                     pl.BlockSpec((2, tm // CHUNK, 1, D), lambda i: (0, i, 0, 0)),
                     pl.BlockSpec((tm, D), row), pl.BlockSpec((tm, 128), row)],
        out_shape=[tok(BF16)] * 4
                  + [jax.ShapeDtypeStruct((T, D), BF16), jax.ShapeDtypeStruct((2, T // CHUNK, 1, D), F32),
                     jax.ShapeDtypeStruct((T, D), F32), jax.ShapeDtypeStruct((T, 128), F32)],
        compiler_params=_cparams(1),
        name="prep",
    )(z_rw, z_rw, z_rw, *consts)


def _intra_kernel(n_chunks, at_ref, bt_ref, kt_ref, rt_ref, v_ref, wc_ref, m_ref, n_ref, rp_ref, y0_ref):
    ri = lax.broadcasted_iota(jnp.int32, (CHUNK, CHUNK), 0)
    ci = lax.broadcasted_iota(jnp.int32, (CHUNK, CHUNK), 1)
    eye = (ri == ci).astype(F32)
    for d in range(2):
        strict = (ri > ci) if d == 0 else (ri < ci)
        incl = (ri >= ci) if d == 0 else (ri <= ci)
        for c in range(n_chunks):
            rs = slice(c * CHUNK, (c + 1) * CHUNK)
            for h in range(2):
                cs = slice(h * HEAD, (h + 1) * HEAD)
                at, bt, kt, rt = at_ref[d, rs, cs], bt_ref[d, rs, cs], kt_ref[d, rs, cs], rt_ref[d, rs, cs]
                v = v_ref[rs, cs]
                wc = wc_ref[d, c, :, cs]
                a_ab = jnp.where(strict, _dot_nt(at, bt), 0.0)
                a_ak = jnp.where(strict, _dot_nt(at, kt), 0.0).astype(BF16)
                a_rb = jnp.where(incl, _dot_nt(rt, bt), 0.0).astype(BF16)
                a_rk = jnp.where(incl, _dot_nt(rt, kt), 0.0).astype(BF16)
                x = a_ab
                t = eye + x
                for _ in range(int(math.log2(CHUNK)) - 1):
                    xb = x.astype(BF16)
                    x = _dot(xb, xb)
                    t = t + _dot(t.astype(BF16), x.astype(BF16))
                tb = t.astype(BF16)
                p = _dot(tb, at).astype(BF16)
                q = _dot(tb, _dot(a_ak, v).astype(BF16)).astype(BF16)
                rp_ref[d, rs, cs] = (rt.astype(F32) + _dot(a_rb, p)).astype(BF16)
                y0_ref[d, rs, cs] = _dot(a_rb, q) + _dot(a_rk, v)
                b_end = (bt.astype(F32) * wc).astype(BF16)
                k_end = (kt.astype(F32) * wc).astype(BF16)
                m_ref[d, rs, cs] = (eye * wc + _dot_tn(b_end, p)).astype(BF16)
                n_ref[d, rs, cs] = _dot_tn(b_end, q) + _dot_tn(k_end, v)


def _intra(at, bt, kt, rt, v, wc, n_chunks):
    T = v.shape[0]
    tm = n_chunks * CHUNK
    lanes = 2 * HEAD
    dblk = pl.BlockSpec((2, tm, lanes), lambda i, j: (0, i, j))
    tok = lambda dt: jax.ShapeDtypeStruct((2, T, D), dt)
    return pl.pallas_call(
        functools.partial(_intra_kernel, n_chunks),
        grid=(T // tm, D // lanes),
        in_specs=[dblk] * 4 + [pl.BlockSpec((tm, lanes), lambda i, j: (i, j)),
                               pl.BlockSpec((2, n_chunks, 1, lanes), lambda i, j: (0, i, 0, j))],
        out_specs=[dblk] * 4,
        out_shape=[tok(BF16), tok(F32), tok(BF16), tok(F32)],
        compiler_params=_cparams(2),
        name="intra",
    )(at, bt, kt, rt, v, wc)


def _scan_kernel(m_ref, n_ref, rp_ref, y0_ref, y_ref, s_ref):
    @pl.when(pl.program_id(2) == 0)
    def _():
        s_ref[...] = jnp.zeros_like(s_ref)

    for h in range(N_HEADS):
        cs = slice(h * HEAD, (h + 1) * HEAD)
        s = s_ref[:, cs]
        sb = s.astype(BF16)
        y_ref[:, cs] = _dot(rp_ref[:, cs], sb) + y0_ref[:, cs]
        s_ref[:, cs] = _dot(m_ref[:, cs], sb) + n_ref[:, cs]


def _scan(m, n, rp, y0, seq):
    T = m.shape[1]
    nc = seq // CHUNK
    idx = lambda d, b, c: (d, b * nc + c + d * (nc - 1 - 2 * c), 0)
    blk = pl.BlockSpec((None, CHUNK, D), idx)
    return pl.pallas_call(
        _scan_kernel,
        grid=(2, T // seq, nc),
        in_specs=[blk] * 4,
        out_specs=blk,
        out_shape=jax.ShapeDtypeStruct((2, T, D), F32),
        scratch_shapes=[pltpu.VMEM((CHUNK, D), F32)],
        compiler_params=_cparams(3, arbitrary=True),
        name="scan",
    )(m, n, rp, y0)


def _mix_kernel(seq, tm, y_ref, bonus_ref, sg_ref, zp_ref, zpprev_ref, zpnext_ref, zg_ref, x_ref,
                lnw_ref, lnb_ref, gup_ref, wrw_ref, poolw_ref, pscale_ref, wpool_ref, bgate_ref, wout_ref,
                bd_ref, o_ref, ext_ref):
    pos0 = (pl.program_id(0) * tm) % seq
    bd = bd_ref[...]

    y = y_ref[0] + y_ref[1]
    mean = _seg_sum(y, bd) * (1.0 / HEAD)
    yc = y - mean
    var = _seg_sum(yc * yc, bd) * (1.0 / HEAD)
    yn = yc * lax.rsqrt(var + GN_EPS) * lnw_ref[...] + lnb_ref[...] + bonus_ref[...]
    gate = _dot(sg_ref[...].astype(BF16), gup_ref[...])
    rwkv_out = _dot((yn * gate).astype(BF16), wrw_ref[...])

    ext_ref[0:POOL_HALO, :] = jnp.where(pos0 == 0, 0.0, zpprev_ref[...])
    ext_ref[POOL_HALO:POOL_HALO + tm, :] = zp_ref[...]
    ext_ref[POOL_HALO + tm:, :] = jnp.where(pos0 + tm == seq, 0.0, zpnext_ref[...])
    pos = pos0 + lax.broadcasted_iota(jnp.int32, (tm, 1), 0)
    pooled = []
    for g, w in enumerate(POOL_WINDOWS):
        cs = slice(g * POOL_GROUP, (g + 1) * POOL_GROUP)
        acc = ext_ref[POOL_HALO - w // 2:POOL_HALO - w // 2 + tm, cs]
        for j in range(1 - w // 2, w // 2):
            acc = acc + ext_ref[POOL_HALO + j:POOL_HALO + j + tm, cs]
        cnt = jnp.minimum(pos + (w // 2 - 1), seq - 1) - jnp.maximum(pos - w // 2, 0) + 1
        pg = acc / cnt.astype(F32) - zp_ref[:, cs]
        pooled.append(_dot(pg.astype(BF16), poolw_ref[g]))
    pooled = jnp.concatenate(pooled, axis=1) * pscale_ref[...]
    pool_out = _dot(pooled.astype(BF16), wpool_ref[...])

    gates = _sigmoid(zg_ref[...] + bgate_ref[...])
    merged = gates[:, :D] * pool_out + gates[:, D:] * rwkv_out
    o_ref[...] = x_ref[...] + _dot(merged.astype(BF16), wout_ref[...])


def _mix(y, bonus, sg, z_pool, z_gate, x, seq, tm, p):
    T = x.shape[0]
    nb = tm // POOL_HALO
    last_blk = T // POOL_HALO - 1
    row = lambda i: (i, 0)
    consts = [p["ln_w"], p["ln_b"], p["g_up"], p["w_rwkv_br"], p["pool_w"], p["pool_scale"], p["w_pool_br"],
              p["b_gate"], p["w_out"], p["bd"]]
    return pl.pallas_call(
        functools.partial(_mix_kernel, seq, tm),
        grid=(T // tm,),
        in_specs=[pl.BlockSpec((2, tm, D), lambda i: (0, i, 0)),
                  pl.BlockSpec((tm, D), row), pl.BlockSpec((tm, 128), row),
                  pl.BlockSpec((tm, POOL_WIDTH), row),
                  pl.BlockSpec((POOL_HALO, POOL_WIDTH), lambda i: (jnp.maximum(i * nb - 1, 0), 0)),
                  pl.BlockSpec((POOL_HALO, POOL_WIDTH), lambda i: (jnp.minimum((i + 1) * nb, last_blk), 0)),
                  pl.BlockSpec((tm, GATE_COLS), row), pl.BlockSpec((tm, D), row)]
                 + [_const_spec(c.shape) for c in consts],
        out_specs=pl.BlockSpec((tm, D), row),
        out_shape=jax.ShapeDtypeStruct((T, D), F32),
        scratch_shapes=[pltpu.VMEM((tm + 2 * POOL_HALO, POOL_WIDTH), F32)],
        compiler_params=_cparams(1),
        name="mix",
    )(y, bonus, sg, z_pool, z_pool, z_pool, z_gate, x, *consts)


def _ffn_kernel(x_ref, gffn_ref, w1_ref, w2_ref, gfin_ref, o_ref):
    x = x_ref[...]
    hn = _rms(x, gffn_ref[...]).astype(BF16)
    h = jnp.maximum(_dot(hn, w1_ref[...]), 0.0)
    x2 = x + _dot((h * h).astype(BF16), w2_ref[...])
    o_ref[...] = _rms(x2, gfin_ref[...])


def _ffn(x, g_ffn, w1, w2, g_final, tm):
    T = x.shape[0]
    row = lambda i: (i, 0)
    return pl.pallas_call(
        _ffn_kernel,
        grid=(T // tm,),
        in_specs=[pl.BlockSpec((tm, D), row), _const_spec((1, D)), _const_spec(w1.shape),
                  _const_spec(w2.shape), _const_spec((1, D))],
        out_specs=pl.BlockSpec((tm, D), row),
        out_shape=jax.ShapeDtypeStruct((T, D), F32),
        compiler_params=_cparams(1),
        name="ffn",
    )(x, g_ffn, w1, w2, g_final)


def _chunk_tri(n, upper):
    i = np.arange(n)
    same = (i[:, None] // CHUNK) == (i[None, :] // CHUNK)
    order = (i[None, :] >= i[:, None]) if upper else (i[None, :] <= i[:, None])
    return jnp.asarray(same & order, BF16)


def _prepare_params(g_mix, w_in, b_gate, mu_prev, mu_next, pool_w, pool_scale, w_pool_br, k_k, k_a, r_k,
                    w0_f, w_up_f, a0_f, a_up_f, w0_b, w_up_b, a0_b, a_up_b, g_up, ln_w, ln_b, w_rwkv_br,
                    w_out, g_ffn, w_ff1, w_ff2, g_final, tm_prep):
    row = lambda a: a.reshape(1, -1).astype(F32)
    lora = w_up_f.shape[0]
    zeros = jnp.zeros((lora, 2 * D), F32)
    seg = np.arange(SEG) // HEAD
    return {
        "g_mix": row(g_mix), "w_in": w_in.astype(BF16), "b_gate": row(b_gate),
        "mu_prev": row(mu_prev), "mu_next": row(mu_next),
        "pool_w": pool_w.astype(BF16), "pool_scale": row(pool_scale), "w_pool_br": w_pool_br.astype(BF16),
        "k_k": row(k_k), "k_a": row(k_a), "r_k": row(r_k),
        "w0": jnp.concatenate([row(w0_f), row(w0_b)], axis=1),
        "a0": jnp.concatenate([row(a0_f), row(a0_b)], axis=1),
        "w_up": jnp.concatenate([jnp.concatenate([w_up_f, w_up_b], axis=1), zeros], axis=0).astype(BF16),
        "a_up": jnp.concatenate([zeros, jnp.concatenate([a_up_f, a_up_b], axis=1)], axis=0).astype(BF16),
        "g_up": g_up.astype(BF16), "ln_w": row(ln_w), "ln_b": row(ln_b),
        "w_rwkv_br": w_rwkv_br.astype(BF16), "w_out": w_out.astype(BF16),
        "g_ffn": row(g_ffn), "w_ff1": w_ff1.astype(BF16), "w_ff2": w_ff2.astype(BF16), "g_final": row(g_final),
        "bd": jnp.asarray(seg[:, None] == seg[None, :], BF16),
        "tri_f": _chunk_tri(tm_prep, upper=False), "tri_b": _chunk_tri(tm_prep, upper=True),
    }


TM_PROJ = 256
TM_PREP = 256
INTRA_CHUNKS = 4
TM_MIX = 256
TM_FFN = 512


def _trunk(x, p):
    batch, seq, _ = x.shape
    xf = x.reshape(batch * seq, D)
    z_pool, z_rw, z_gate = _in_proj(xf, p["g_mix"], p["w_in"], TM_PROJ)
    at, bt, kt, rt, v, wc, bonus, sg = _prep(z_rw, seq, TM_PREP, p)
    m, n, rp, y0 = _intra(at, bt, kt, rt, v, wc, INTRA_CHUNKS)
    y = _scan(m, n, rp, y0, seq)
    x1 = _mix(y, bonus, sg, z_pool, z_gate, xf, seq, TM_MIX, p)
    out = _ffn(x1, p["g_ffn"], p["w_ff1"], p["w_ff2"], p["g_final"], TM_FFN)
    return out.reshape(batch, seq, D)


def kernel(x_prompt, x_sample, g_mix, w_in, b_gate, mu_prev, mu_next, pool_w, pool_scale, w_pool_br, k_k, k_a, r_k, w0_f, w_up_f, a0_f, a_up_f, w0_b, w_up_b, a0_b, a_up_b, g_up, ln_w, ln_b, w_rwkv_br, w_out, g_ffn, w_ff1, w_ff2, g_final):
    depth = g_mix.shape[0]
    layers = [_prepare_params(g_mix[l], w_in[l], b_gate[l], mu_prev[l], mu_next[l], pool_w[l], pool_scale[l],
                              w_pool_br[l], k_k[l], k_a[l], r_k[l], w0_f[l], w_up_f[l], a0_f[l], a_up_f[l],
                              w0_b[l], w_up_b[l], a0_b[l], a_up_b[l], g_up[l], ln_w[l], ln_b[l], w_rwkv_br[l],
                              w_out[l], g_ffn[l], w_ff1[l], w_ff2[l], g_final, TM_PREP) for l in range(depth)]
    assert depth == 1, "the final norm is fused into the last layer's ffn; only depth 1 is supported"
    return tuple(_trunk(x, layers[0]) for x in (x_prompt, x_sample))
```

```python
import functools
import math

import jax
import jax.numpy as jnp
import numpy as np
from jax import lax
from jax.experimental import pallas as pl
from jax.experimental.pallas import tpu as pltpu

D = 1024
HEAD = 64
N_HEADS = D // HEAD
POOL_WIDTH = 512
POOL_GROUP = 128
POOL_WINDOWS = (2, 4, 8, 16)
POOL_HALO = 8
RW_COLS = 3 * D + 256
GATE_COLS = 2 * D
D_FF = 4 * D
RMS_EPS = 1e-6
GN_EPS = 64e-5
L2_EPS = 1e-12
CHUNK = 64
SEG = 256
VMEM_LIMIT = 56 * 1024 * 1024

F32 = jnp.float32
BF16 = jnp.bfloat16


def _dot(a, b):
    return jnp.dot(a, b, preferred_element_type=F32)


def _dot_nt(a, b):
    return lax.dot_general(a, b, (((1,), (1,)), ((), ())), preferred_element_type=F32)


def _dot_tn(a, b):
    return lax.dot_general(a, b, (((0,), (0,)), ((), ())), preferred_element_type=F32)


def _split(x):
    hi = x.astype(BF16)
    lo = (x - hi.astype(F32)).astype(BF16)
    return hi, lo


def _seg_sum(x, bd):
    outs = []
    for g in range(D // SEG):
        hi, lo = _split(x[:, g * SEG:(g + 1) * SEG])
        outs.append(_dot(hi, bd) + _dot(lo, bd))
    return jnp.concatenate(outs, axis=1)


def _rms(x, g):
    return x * lax.rsqrt(jnp.mean(x * x, axis=-1, keepdims=True) + RMS_EPS) * g


def _sigmoid(x):
    return 1.0 / (1.0 + jnp.exp(-x))


def _cparams(n_axes, arbitrary=False):
    sem = ("arbitrary",) * n_axes if arbitrary else ("parallel",) * n_axes
    return pltpu.CompilerParams(dimension_semantics=sem, vmem_limit_bytes=VMEM_LIMIT)


def _const_spec(shape):
    nd = len(shape)
    return pl.BlockSpec(shape, lambda *_: (0,) * nd, pipeline_mode=pl.Buffered(1))


def _in_proj_kernel(x_ref, g_ref, w_ref, zp_ref, zr_ref, zg_ref):
    xn = _rms(x_ref[...], g_ref[...]).astype(BF16)
    zp_ref[...] = _dot(xn, w_ref[:, 0:POOL_WIDTH])
    zr_ref[...] = _dot(xn, w_ref[:, POOL_WIDTH:POOL_WIDTH + RW_COLS])
    zg_ref[...] = _dot(xn, w_ref[:, POOL_WIDTH + RW_COLS:])


def _in_proj(x, g_mix, w_in, tm):
    T = x.shape[0]
    row = lambda i: (i, 0)
    return pl.pallas_call(
        _in_proj_kernel,
        grid=(T // tm,),
        in_specs=[pl.BlockSpec((tm, D), row), _const_spec((1, D)), _const_spec(w_in.shape)],
        out_specs=[pl.BlockSpec((tm, POOL_WIDTH), row), pl.BlockSpec((tm, RW_COLS), row),
                   pl.BlockSpec((tm, GATE_COLS), row)],
        out_shape=[jax.ShapeDtypeStruct((T, POOL_WIDTH), F32), jax.ShapeDtypeStruct((T, RW_COLS), F32),
                   jax.ShapeDtypeStruct((T, GATE_COLS), F32)],
        compiler_params=_cparams(1),
        name="in_proj",
    )(x, g_mix, w_in)


def _prep_kernel(seq, tm, z_ref, zprev_ref, znext_ref, mup_ref, mun_ref, kk_ref, ka_ref, rk_ref,
                 w0_ref, a0_ref, wup_ref, aup_ref, bd_ref, trif_ref, trib_ref,
                 at_ref, bt_ref, kt_ref, rt_ref, v_ref, wc_ref, bonus_ref, sg_ref):
    pos0 = (pl.program_id(0) * tm) % seq
    at_start = pos0 == 0
    at_end = pos0 + tm == seq
    rows = lax.broadcasted_iota(jnp.int32, (tm, 1), 0)

    def mix(c0, c1):
        z = z_ref[:, c0:c1]
        prev_row = jnp.where(at_start, 0.0, zprev_ref[POOL_HALO - 1:POOL_HALO, c0:c1])
        next_row = jnp.where(at_end, 0.0, znext_ref[0:1, c0:c1])
        z_prev = jnp.where(rows == 0, prev_row, pltpu.roll(z, 1, axis=0))
        z_next = jnp.where(rows == tm - 1, next_row, pltpu.roll(z, tm - 1, axis=0))
        return z + mup_ref[:, c0:c1] * (z_prev - z) + mun_ref[:, c0:c1] * (z_next - z)

    r = mix(0, D)
    k = mix(D, 2 * D)
    v = mix(2 * D, 3 * D)
    zwa = mix(3 * D, 3 * D + 128)
    zg = mix(3 * D + 128, 3 * D + 256)
    sg_ref[...] = _sigmoid(zg)
    v_ref[...] = v.astype(BF16)

    w_raw = w0_ref[...] + _dot(jnp.tanh(zwa).astype(BF16), wup_ref[...])
    a_all = _sigmoid(a0_ref[...] + _dot(zwa.astype(BF16), aup_ref[...]))
    lw_all = -_sigmoid(w_raw) * math.exp(-0.5)

    bd = bd_ref[...]
    kkr = k * kk_ref[...]
    ss = _seg_sum(kkr * kkr, bd)
    kk = kkr * jnp.minimum(lax.rsqrt(ss), 1.0 / L2_EPS)

    k_sum = jnp.zeros_like(k)
    for d, tri_ref in enumerate((trif_ref, trib_ref)):
        a = a_all[:, d * D:(d + 1) * D]
        lw = lw_all[:, d * D:(d + 1) * D]
        kd = k * (1.0 + (a - 1.0) * ka_ref[...])
        k_sum = k_sum + kd
        hi, lo = _split(lw)
        tri = tri_ref[...]
        cum = _dot(tri, hi) + _dot(tri, lo)
        e_neg = jnp.exp(-cum)
        rt_ref[d] = (r * jnp.exp(cum)).astype(BF16)
        bt_ref[d] = (kk * a * e_neg).astype(BF16)
        kt_ref[d] = (kd * e_neg).astype(BF16)
        at_ref[d] = (-kk * jnp.exp(cum - lw)).astype(BF16)
        for c in range(tm // CHUNK):
            last = c * CHUNK + (CHUNK - 1 if d == 0 else 0)
            wc_ref[d, c] = jnp.exp(cum[last:last + 1, :])
    bonus_ref[...] = _seg_sum(r * k_sum * rk_ref[...], bd) * v


def _prep(z_rw, seq, tm, p):
    T = z_rw.shape[0]
    nb = tm // POOL_HALO
    last_blk = T // POOL_HALO - 1
    row = lambda i: (i, 0)
    drow = lambda i: (0, i, 0)
    consts = [p["mu_prev"], p["mu_next"], p["k_k"], p["k_a"], p["r_k"], p["w0"], p["a0"], p["w_up"],
              p["a_up"], p["bd"], p["tri_f"], p["tri_b"]]
    tok = lambda dt: jax.ShapeDtypeStruct((2, T, D), dt)
    return pl.pallas_call(
        functools.partial(_prep_kernel, seq, tm),
        grid=(T // tm,),
        in_specs=[pl.BlockSpec((tm, RW_COLS), row),
                  pl.BlockSpec((POOL_HALO, RW_COLS), lambda i: (jnp.maximum(i * nb - 1, 0), 0)),
                  pl.BlockSpec((POOL_HALO, RW_COLS), lambda i: (jnp.minimum((i + 1) * nb, last_blk), 0))]
                 + [_const_spec(c.shape) for c in consts],
        out_specs=[pl.BlockSpec((2, tm, D), drow)] * 4
                  + [pl.BlockSpec((tm, D), row),
                     pl.BlockSpec((2, tm // CHUNK, 1, D), lambda i: (0, i, 0, 0)),
                     pl.BlockSpec((tm, D), row), pl.BlockSpec((tm, 128), row)],
        out_shape=[tok(BF16)] * 4
                  + [jax.ShapeDtypeStruct((T, D), BF16), jax.ShapeDtypeStruct((2, T // CHUNK, 1, D), F32),
                     jax.ShapeDtypeStruct((T, D), F32), jax.ShapeDtypeStruct((T, 128), F32)],
        compiler_params=_cparams(1),
        name="prep",
    )(z_rw, z_rw, z_rw, *consts)


def _bd(x, lo):
    zero = jnp.zeros_like(x)
    return jnp.concatenate([jnp.where(lo, x, zero), jnp.where(lo, zero, x)], axis=0)


def _undiag(full, lo):
    return jnp.where(lo, full[:CHUNK], full[CHUNK:])


def _intra_kernel(n_chunks, at_ref, bt_ref, kt_ref, rt_ref, v_ref, wc_ref, m_ref, n_ref, rp_ref, y0_ref):
    row = lax.broadcasted_iota(jnp.int32, (CHUNK, 2 * HEAD), 0)
    lane = lax.broadcasted_iota(jnp.int32, (CHUNK, 2 * HEAD), 1)
    lo = lane < HEAD
    col = jnp.where(lo, lane, lane - HEAD)
    eye = (row == col).astype(F32)
    insts = [(d, c) for d in range(2) for c in range(n_chunks)]
    rows = lambda c: slice(c * CHUNK, (c + 1) * CHUNK)

    xs, ts, a_ak, a_rb, a_rk = [], [], [], [], []
    for d, c in insts:
        strict = (row > col) if d == 0 else (row < col)
        incl = (row >= col) if d == 0 else (row <= col)
        lhs = jnp.concatenate([at_ref[d, rows(c), :], rt_ref[d, rows(c), :]], axis=0)
        rhs = jnp.concatenate([_bd(bt_ref[d, rows(c), :], lo), _bd(kt_ref[d, rows(c), :], lo)], axis=0)
        out = _dot_nt(lhs, rhs)
        x = jnp.where(strict, out[:CHUNK, :128], 0.0)
        xs.append(x)
        ts.append(eye + x)
        a_ak.append(jnp.where(strict, out[:CHUNK, 128:], 0.0).astype(BF16))
        a_rb.append(jnp.where(incl, out[CHUNK:, :128], 0.0).astype(BF16))
        a_rk.append(jnp.where(incl, out[CHUNK:, 128:], 0.0).astype(BF16))

    n_steps = int(math.log2(CHUNK))
    for j in range(n_steps):
        for i in range(len(insts)):
            xb = xs[i].astype(BF16)
            rhs = _bd(xb, lo)
            if j == 0:
                xs[i] = _dot(xb, rhs)
            elif j < n_steps - 1:
                out = _dot(jnp.concatenate([xb, ts[i].astype(BF16)], axis=0), rhs)
                xs[i] = out[:CHUNK]
                ts[i] = ts[i] + out[CHUNK:]
            else:
                ts[i] = ts[i] + _dot(ts[i].astype(BF16), rhs)

    akv, arkv = [], []
    for i, (d, c) in enumerate(insts):
        out = _dot(jnp.concatenate([a_ak[i], a_rk[i]], axis=0), _bd(v_ref[rows(c), :], lo))
        akv.append(out[:CHUNK].astype(BF16))
        arkv.append(out[CHUNK:])

    ps, qs = [], []
    for i, (d, c) in enumerate(insts):
        rhs = jnp.concatenate([_bd(at_ref[d, rows(c), :], lo), _bd(akv[i], lo)], axis=1)
        pq = _dot(ts[i].astype(BF16), rhs)
        ps.append(pq[:, :128].astype(BF16))
        qs.append(pq[:, 128:].astype(BF16))

    for i, (d, c) in enumerate(insts):
        ry = _dot(a_rb[i], jnp.concatenate([_bd(ps[i], lo), _bd(qs[i], lo)], axis=1))
        rp_ref[d, rows(c), :] = (rt_ref[d, rows(c), :].astype(F32) + ry[:, :128]).astype(BF16)
        y0_ref[d, rows(c), :] = ry[:, 128:] + arkv[i]

    for i, (d, c) in enumerate(insts):
        wc = wc_ref[d, c]
        b_end = (bt_ref[d, rows(c), :].astype(F32) * wc).astype(BF16)
        k_end = (kt_ref[d, rows(c), :].astype(F32) * wc).astype(BF16)
        lhs_t = jnp.concatenate([b_end, k_end], axis=0)
        v = v_ref[rows(c), :]
        rhs = jnp.concatenate([jnp.concatenate([ps[i], qs[i]], axis=1),
                               jnp.concatenate([jnp.zeros_like(v), v], axis=1)], axis=0)
        full = _dot_tn(lhs_t, rhs)
        m_ref[d, rows(c), :] = (eye * wc + _undiag(full[:, :128], lo)).astype(BF16)
        n_ref[d, rows(c), :] = _undiag(full[:, 128:], lo)


def _intra(at, bt, kt, rt, v, wc, n_chunks):
    T = v.shape[0]
    tm = n_chunks * CHUNK
    lanes = 2 * HEAD
    dblk = pl.BlockSpec((2, tm, lanes), lambda i, j: (0, i, j))
    tok = lambda dt: jax.ShapeDtypeStruct((2, T, D), dt)
    return pl.pallas_call(
        functools.partial(_intra_kernel, n_chunks),
        grid=(T // tm, D // lanes),
        in_specs=[dblk] * 4 + [pl.BlockSpec((tm, lanes), lambda i, j: (i, j)),
                               pl.BlockSpec((2, n_chunks, 1, lanes), lambda i, j: (0, i, 0, j))],
        out_specs=[dblk] * 4,
        out_shape=[tok(BF16), tok(F32), tok(BF16), tok(F32)],
        compiler_params=_cparams(2),
        name="intra",
    )(at, bt, kt, rt, v, wc)


def _scan_kernel(n_chunks, mf_ref, nf_ref, rpf_ref, y0f_ref, mb_ref, nb_ref, rpb_ref, y0b_ref,
                 yf_ref, yb_ref, s_ref):
    @pl.when(pl.program_id(1) == 0)
    def _():
        s_ref[...] = jnp.zeros_like(s_ref)

    lo = lax.broadcasted_iota(jnp.int32, (CHUNK, 2 * HEAD), 1) < HEAD
    n_pairs = D // (2 * HEAD)
    dirs = ((mf_ref, nf_ref, rpf_ref, y0f_ref, yf_ref), (mb_ref, nb_ref, rpb_ref, y0b_ref, yb_ref))
    state = [[s_ref[d, :, g * 128:(g + 1) * 128] for g in range(n_pairs)] for d in range(2)]
    for step in range(n_chunks):
        for d, (m_ref, n_ref, rp_ref, y0_ref, y_ref) in enumerate(dirs):
            c = step if d == 0 else n_chunks - 1 - step
            rs = slice(c * CHUNK, (c + 1) * CHUNK)
            for g in range(n_pairs):
                cs = slice(g * 128, (g + 1) * 128)
                lhs = jnp.concatenate([m_ref[rs, cs], rp_ref[rs, cs]], axis=0)
                out = _dot(lhs, _bd(state[d][g].astype(BF16), lo))
                y_ref[rs, cs] = out[CHUNK:] + y0_ref[rs, cs]
                state[d][g] = out[:CHUNK] + n_ref[rs, cs]
    for d in range(2):
        for g in range(n_pairs):
            s_ref[d, :, g * 128:(g + 1) * 128] = state[d][g]


def _scan(m, n, rp, y0, seq, n_chunks):
    T = m.shape[1]
    tm = n_chunks * CHUNK
    nblk = seq // tm
    fwd = pl.BlockSpec((None, tm, D), lambda b, j: (0, b * nblk + j, 0))
    bwd = pl.BlockSpec((None, tm, D), lambda b, j: (1, b * nblk + nblk - 1 - j, 0))
    return pl.pallas_call(
        functools.partial(_scan_kernel, n_chunks),
        grid=(T // seq, nblk),
        in_specs=[fwd] * 4 + [bwd] * 4,
        out_specs=[pl.BlockSpec((tm, D), lambda b, j: (b * nblk + j, 0)),
                   pl.BlockSpec((tm, D), lambda b, j: (b * nblk + nblk - 1 - j, 0))],
        out_shape=[jax.ShapeDtypeStruct((T, D), F32)] * 2,
        scratch_shapes=[pltpu.VMEM((2, CHUNK, D), F32)],
        compiler_params=_cparams(2, arbitrary=True),
        name="scan",
    )(m, n, rp, y0, m, n, rp, y0)


def _mix_kernel(seq, tm, yf_ref, yb_ref, bonus_ref, sg_ref, zp_ref, zpprev_ref, zpnext_ref, zg_ref, x_ref,
                lnw_ref, lnb_ref, gup_ref, wrw_ref, poolw_ref, pscale_ref, wpool_ref, bgate_ref, wout_ref,
                bd_ref, o_ref, ext_ref):
    pos0 = (pl.program_id(0) * tm) % seq
    bd = bd_ref[...]

    y = yf_ref[...] + yb_ref[...]
    mean = _seg_sum(y, bd) * (1.0 / HEAD)
    yc = y - mean
    var = _seg_sum(yc * yc, bd) * (1.0 / HEAD)
    yn = yc * lax.rsqrt(var + GN_EPS) * lnw_ref[...] + lnb_ref[...] + bonus_ref[...]
    gate = _dot(sg_ref[...].astype(BF16), gup_ref[...])
    rwkv_out = _dot((yn * gate).astype(BF16), wrw_ref[...])

    ext_ref[0:POOL_HALO, :] = jnp.where(pos0 == 0, 0.0, zpprev_ref[...])
    ext_ref[POOL_HALO:POOL_HALO + tm, :] = zp_ref[...]
    ext_ref[POOL_HALO + tm:, :] = jnp.where(pos0 + tm == seq, 0.0, zpnext_ref[...])
    pos = pos0 + lax.broadcasted_iota(jnp.int32, (tm, 1), 0)
    pooled = []
    for g, w in enumerate(POOL_WINDOWS):
        cs = slice(g * POOL_GROUP, (g + 1) * POOL_GROUP)
        acc = ext_ref[POOL_HALO - w // 2:POOL_HALO - w // 2 + tm, cs]
        for j in range(1 - w // 2, w // 2):
            acc = acc + ext_ref[POOL_HALO + j:POOL_HALO + j + tm, cs]
        cnt = jnp.minimum(pos + (w // 2 - 1), seq - 1) - jnp.maximum(pos - w // 2, 0) + 1
        pg = acc / cnt.astype(F32) - zp_ref[:, cs]
        pooled.append(_dot(pg.astype(BF16), poolw_ref[g]))
    pooled = jnp.concatenate(pooled, axis=1) * pscale_ref[...]
    pool_out = _dot(pooled.astype(BF16), wpool_ref[...])

    gates = _sigmoid(zg_ref[...] + bgate_ref[...])
    merged = gates[:, :D] * pool_out + gates[:, D:] * rwkv_out
    o_ref[...] = x_ref[...] + _dot(merged.astype(BF16), wout_ref[...])


def _mix(y_f, y_b, bonus, sg, z_pool, z_gate, x, seq, tm, p):
    T = x.shape[0]
    nb = tm // POOL_HALO
    last_blk = T // POOL_HALO - 1
    row = lambda i: (i, 0)
    consts = [p["ln_w"], p["ln_b"], p["g_up"], p["w_rwkv_br"], p["pool_w"], p["pool_scale"], p["w_pool_br"],
              p["b_gate"], p["w_out"], p["bd"]]
    return pl.pallas_call(
        functools.partial(_mix_kernel, seq, tm),
        grid=(T // tm,),
        in_specs=[pl.BlockSpec((tm, D), row), pl.BlockSpec((tm, D), row),
                  pl.BlockSpec((tm, D), row), pl.BlockSpec((tm, 128), row),
                  pl.BlockSpec((tm, POOL_WIDTH), row),
                  pl.BlockSpec((POOL_HALO, POOL_WIDTH), lambda i: (jnp.maximum(i * nb - 1, 0), 0)),
                  pl.BlockSpec((POOL_HALO, POOL_WIDTH), lambda i: (jnp.minimum((i + 1) * nb, last_blk), 0)),
                  pl.BlockSpec((tm, GATE_COLS), row), pl.BlockSpec((tm, D), row)]
                 + [_const_spec(c.shape) for c in consts],
        out_specs=pl.BlockSpec((tm, D), row),
        out_shape=jax.ShapeDtypeStruct((T, D), F32),
        scratch_shapes=[pltpu.VMEM((tm + 2 * POOL_HALO, POOL_WIDTH), F32)],
        compiler_params=_cparams(1),
        name="mix",
    )(y_f, y_b, bonus, sg, z_pool, z_pool, z_pool, z_gate, x, *consts)


def _ffn_kernel(x_ref, gffn_ref, w1_ref, w2_ref, gfin_ref, o_ref):
    x = x_ref[...]
    hn = _rms(x, gffn_ref[...]).astype(BF16)
    h = jnp.maximum(_dot(hn, w1_ref[...]), 0.0)
    x2 = x + _dot((h * h).astype(BF16), w2_ref[...])
    o_ref[...] = _rms(x2, gfin_ref[...])


def _ffn(x, g_ffn, w1, w2, g_final, tm):
    T = x.shape[0]
    row = lambda i: (i, 0)
    return pl.pallas_call(
        _ffn_kernel,
        grid=(T // tm,),
        in_specs=[pl.BlockSpec((tm, D), row), _const_spec((1, D)), _const_spec(w1.shape),
                  _const_spec(w2.shape), _const_spec((1, D))],
        out_specs=pl.BlockSpec((tm, D), row),
        out_shape=jax.ShapeDtypeStruct((T, D), F32),
        compiler_params=_cparams(1),
        name="ffn",
    )(x, g_ffn, w1, w2, g_final)


def _chunk_tri(n, upper):
    i = np.arange(n)
    same = (i[:, None] // CHUNK) == (i[None, :] // CHUNK)
    order = (i[None, :] >= i[:, None]) if upper else (i[None, :] <= i[:, None])
    return jnp.asarray(same & order, BF16)


def _prepare_params(g_mix, w_in, b_gate, mu_prev, mu_next, pool_w, pool_scale, w_pool_br, k_k, k_a, r_k,
                    w0_f, w_up_f, a0_f, a_up_f, w0_b, w_up_b, a0_b, a_up_b, g_up, ln_w, ln_b, w_rwkv_br,
                    w_out, g_ffn, w_ff1, w_ff2, g_final, tm_prep):
    row = lambda a: a.reshape(1, -1).astype(F32)
    lora = w_up_f.shape[0]
    zeros = jnp.zeros((lora, 2 * D), F32)
    seg = np.arange(SEG) // HEAD
    return {
        "g_mix": row(g_mix), "w_in": w_in.astype(BF16), "b_gate": row(b_gate),
        "mu_prev": row(mu_prev), "mu_next": row(mu_next),
        "pool_w": pool_w.astype(BF16), "pool_scale": row(pool_scale), "w_pool_br": w_pool_br.astype(BF16),
        "k_k": row(k_k), "k_a": row(k_a), "r_k": row(r_k),
        "w0": jnp.concatenate([row(w0_f), row(w0_b)], axis=1),
        "a0": jnp.concatenate([row(a0_f), row(a0_b)], axis=1),
        "w_up": jnp.concatenate([jnp.concatenate([w_up_f, w_up_b], axis=1), zeros], axis=0).astype(BF16),
        "a_up": jnp.concatenate([zeros, jnp.concatenate([a_up_f, a_up_b], axis=1)], axis=0).astype(BF16),
        "g_up": g_up.astype(BF16), "ln_w": row(ln_w), "ln_b": row(ln_b),
        "w_rwkv_br": w_rwkv_br.astype(BF16), "w_out": w_out.astype(BF16),
        "g_ffn": row(g_ffn), "w_ff1": w_ff1.astype(BF16), "w_ff2": w_ff2.astype(BF16), "g_final": row(g_final),
        "bd": jnp.asarray(seg[:, None] == seg[None, :], BF16),
        "tri_f": _chunk_tri(tm_prep, upper=False), "tri_b": _chunk_tri(tm_prep, upper=True),
    }


TM_PROJ = 256
TM_PREP = 256
INTRA_CHUNKS = 4
SCAN_CHUNKS = 4
TM_MIX = 256
TM_FFN = 512


def _trunk(x, p):
    batch, seq, _ = x.shape
    xf = x.reshape(batch * seq, D)
    z_pool, z_rw, z_gate = _in_proj(xf, p["g_mix"], p["w_in"], TM_PROJ)
    at, bt, kt, rt, v, wc, bonus, sg = _prep(z_rw, seq, TM_PREP, p)
    m, n, rp, y0 = _intra(at, bt, kt, rt, v, wc, INTRA_CHUNKS)
    y_f, y_b = _scan(m, n, rp, y0, seq, SCAN_CHUNKS)
    x1 = _mix(y_f, y_b, bonus, sg, z_pool, z_gate, xf, seq, TM_MIX, p)
    out = _ffn(x1, p["g_ffn"], p["w_ff1"], p["w_ff2"], p["g_final"], TM_FFN)
    return out.reshape(batch, seq, D)


def kernel(x_prompt, x_sample, g_mix, w_in, b_gate, mu_prev, mu_next, pool_w, pool_scale, w_pool_br, k_k, k_a, r_k, w0_f, w_up_f, a0_f, a_up_f, w0_b, w_up_b, a0_b, a_up_b, g_up, ln_w, ln_b, w_rwkv_br, w_out, g_ffn, w_ff1, w_ff2, g_final):
    depth = g_mix.shape[0]
    layers = [_prepare_params(g_mix[l], w_in[l], b_gate[l], mu_prev[l], mu_next[l], pool_w[l], pool_scale[l],
                              w_pool_br[l], k_k[l], k_a[l], r_k[l], w0_f[l], w_up_f[l], a0_f[l], a_up_f[l],
                              w0_b[l], w_up_b[l], a0_b[l], a_up_b[l], g_up[l], ln_w[l], ln_b[l], w_rwkv_br[l],
                              w_out[l], g_ffn[l], w_ff1[l], w_ff2[l], g_final, TM_PREP) for l in range(depth)]
    assert depth == 1, "the final norm is fused into the last layer's ffn; only depth 1 is supported"
    return tuple(_trunk(x, layers[0]) for x in (x_prompt, x_sample))
```

```python
import functools
import math

import jax
import jax.numpy as jnp
import numpy as np
from jax import lax
from jax.experimental import pallas as pl
from jax.experimental.pallas import tpu as pltpu

D = 1024
HEAD = 64
N_HEADS = D // HEAD
POOL_WIDTH = 512
POOL_GROUP = 128
POOL_WINDOWS = (2, 4, 8, 16)
POOL_HALO = 8
RW_COLS = 3 * D + 256
GATE_COLS = 2 * D
D_FF = 4 * D
RMS_EPS = 1e-6
GN_EPS = 64e-5
L2_EPS = 1e-12
CHUNK = 64
SEG = 256
VMEM_LIMIT = 56 * 1024 * 1024

F32 = jnp.float32
BF16 = jnp.bfloat16


def _dot(a, b):
    return jnp.dot(a, b, preferred_element_type=F32)


def _dot_nt(a, b):
    return lax.dot_general(a, b, (((1,), (1,)), ((), ())), preferred_element_type=F32)


def _dot_tn(a, b):
    return lax.dot_general(a, b, (((0,), (0,)), ((), ())), preferred_element_type=F32)


def _split(x):
    hi = x.astype(BF16)
    lo = (x - hi.astype(F32)).astype(BF16)
    return hi, lo


def _seg_sum(x, bd):
    outs = []
    for g in range(D // SEG):
        hi, lo = _split(x[:, g * SEG:(g + 1) * SEG])
        outs.append(_dot(hi, bd) + _dot(lo, bd))
    return jnp.concatenate(outs, axis=1)


def _rms(x, g):
    return x * lax.rsqrt(jnp.mean(x * x, axis=-1, keepdims=True) + RMS_EPS) * g


def _sigmoid(x):
    return 1.0 / (1.0 + jnp.exp(-x))


def _cparams(n_axes, arbitrary=False):
    sem = ("arbitrary",) * n_axes if arbitrary else ("parallel",) * n_axes
    return pltpu.CompilerParams(dimension_semantics=sem, vmem_limit_bytes=VMEM_LIMIT)


def _const_spec(shape):
    nd = len(shape)
    return pl.BlockSpec(shape, lambda *_: (0,) * nd, pipeline_mode=pl.Buffered(1))


def _in_proj_kernel(x_ref, g_ref, w_ref, zp_ref, zr_ref, zg_ref):
    xn = _rms(x_ref[...], g_ref[...]).astype(BF16)
    zp_ref[...] = _dot(xn, w_ref[:, 0:POOL_WIDTH])
    zr_ref[...] = _dot(xn, w_ref[:, POOL_WIDTH:POOL_WIDTH + RW_COLS])
    zg_ref[...] = _dot(xn, w_ref[:, POOL_WIDTH + RW_COLS:])


def _in_proj(x, g_mix, w_in, tm):
    T = x.shape[0]
    row = lambda i: (i, 0)
    return pl.pallas_call(
        _in_proj_kernel,
        grid=(T // tm,),
        in_specs=[pl.BlockSpec((tm, D), row), _const_spec((1, D)), _const_spec(w_in.shape)],
        out_specs=[pl.BlockSpec((tm, POOL_WIDTH), row), pl.BlockSpec((tm, RW_COLS), row),
                   pl.BlockSpec((tm, GATE_COLS), row)],
        out_shape=[jax.ShapeDtypeStruct((T, POOL_WIDTH), F32), jax.ShapeDtypeStruct((T, RW_COLS), F32),
                   jax.ShapeDtypeStruct((T, GATE_COLS), F32)],
        compiler_params=_cparams(1),
        name="in_proj",
    )(x, g_mix, w_in)


def _prep_kernel(seq, tm, z_ref, zprev_ref, znext_ref, mup_ref, mun_ref, kk_ref, ka_ref, rk_ref,
                 w0_ref, a0_ref, wup_ref, aup_ref, bd_ref, trif_ref, trib_ref,
                 at_ref, bt_ref, kt_ref, rt_ref, v_ref, wc_ref, bonus_ref, sg_ref):
    pos0 = (pl.program_id(0) * tm) % seq
    at_start = pos0 == 0
    at_end = pos0 + tm == seq
    rows = lax.broadcasted_iota(jnp.int32, (tm, 1), 0)

    def mix(c0, c1):
        z = z_ref[:, c0:c1]
        prev_row = jnp.where(at_start, 0.0, zprev_ref[POOL_HALO - 1:POOL_HALO, c0:c1])
        next_row = jnp.where(at_end, 0.0, znext_ref[0:1, c0:c1])
        z_prev = jnp.where(rows == 0, prev_row, pltpu.roll(z, 1, axis=0))
        z_next = jnp.where(rows == tm - 1, next_row, pltpu.roll(z, tm - 1, axis=0))
        return z + mup_ref[:, c0:c1] * (z_prev - z) + mun_ref[:, c0:c1] * (z_next - z)

    r = mix(0, D)
    k = mix(D, 2 * D)
    v = mix(2 * D, 3 * D)
    zwa = mix(3 * D, 3 * D + 128)
    zg = mix(3 * D + 128, 3 * D + 256)
    sg_ref[...] = _sigmoid(zg)
    v_ref[...] = v.astype(BF16)

    w_raw = w0_ref[...] + _dot(jnp.tanh(zwa).astype(BF16), wup_ref[...])
    a_all = _sigmoid(a0_ref[...] + _dot(zwa.astype(BF16), aup_ref[...]))
    lw_all = -_sigmoid(w_raw) * math.exp(-0.5)

    bd = bd_ref[...]
    kkr = k * kk_ref[...]
    ss = _seg_sum(kkr * kkr, bd)
    kk = kkr * jnp.minimum(lax.rsqrt(ss), 1.0 / L2_EPS)

    k_sum = jnp.zeros_like(k)
    for d, tri_ref in enumerate((trif_ref, trib_ref)):
        a = a_all[:, d * D:(d + 1) * D]
        lw = lw_all[:, d * D:(d + 1) * D]
        kd = k * (1.0 + (a - 1.0) * ka_ref[...])
        k_sum = k_sum + kd
        hi, lo = _split(lw)
        tri = tri_ref[...]
        cum = _dot(tri, hi) + _dot(tri, lo)
        e_neg = jnp.exp(-cum)
        rt_ref[d] = (r * jnp.exp(cum)).astype(BF16)
        bt_ref[d] = (kk * a * e_neg).astype(BF16)
        kt_ref[d] = (kd * e_neg).astype(BF16)
        at_ref[d] = (-kk * jnp.exp(cum - lw)).astype(BF16)
        for c in range(tm // CHUNK):
            last = c * CHUNK + (CHUNK - 1 if d == 0 else 0)
            wc_ref[d, c] = jnp.exp(cum[last:last + 1, :])
    bonus_ref[...] = _seg_sum(r * k_sum * rk_ref[...], bd) * v


def _prep(z_rw, seq, tm, p):
    T = z_rw.shape[0]
    nb = tm // POOL_HALO
    last_blk = T // POOL_HALO - 1
    row = lambda i: (i, 0)
    drow = lambda i: (0, i, 0)
    consts = [p["mu_prev"], p["mu_next"], p["k_k"], p["k_a"], p["r_k"], p["w0"], p["a0"], p["w_up"],
              p["a_up"], p["bd"], p["tri_f"], p["tri_b"]]
    tok = lambda dt: jax.ShapeDtypeStruct((2, T, D), dt)
    return pl.pallas_call(
        functools.partial(_prep_kernel, seq, tm),
        grid=(T // tm,),
        in_specs=[pl.BlockSpec((tm, RW_COLS), row),
                  pl.BlockSpec((POOL_HALO, RW_COLS), lambda i: (jnp.maximum(i * nb - 1, 0), 0)),
                  pl.BlockSpec((POOL_HALO, RW_COLS), lambda i: (jnp.minimum((i + 1) * nb, last_blk), 0))]
                 + [_const_spec(c.shape) for c in consts],
        out_specs=[pl.BlockSpec((2, tm, D), drow)] * 4
                  + [pl.BlockSpec((tm, D), row),
                     pl.BlockSpec((2, tm // CHUNK, 1, D), lambda i: (0, i, 0, 0)),
                     pl.BlockSpec((tm, D), row), pl.BlockSpec((tm, 128), row)],
        out_shape=[tok(BF16)] * 4
                  + [jax.ShapeDtypeStruct((T, D), BF16), jax.ShapeDtypeStruct((2, T // CHUNK, 1, D), F32),
                     jax.ShapeDtypeStruct((T, D), F32), jax.ShapeDtypeStruct((T, 128), F32)],
        compiler_params=_cparams(1),
        name="prep",
    )(z_rw, z_rw, z_rw, *consts)


def _bd(x, lo):
    zero = jnp.zeros_like(x)
    return jnp.concatenate([jnp.where(lo, x, zero), jnp.where(lo, zero, x)], axis=0)


def _undiag(full, lo):
    return jnp.where(lo, full[:CHUNK], full[CHUNK:])


def _wkv_kernel(n_pairs, atf_ref, btf_ref, ktf_ref, rtf_ref, vf_ref, wcf_ref,
                atb_ref, btb_ref, ktb_ref, rtb_ref, vb_ref, wcb_ref, yf_ref, yb_ref, s_ref):
    @pl.when(pl.program_id(2) == 0)
    def _():
        s_ref[...] = jnp.zeros_like(s_ref)

    row = lax.broadcasted_iota(jnp.int32, (CHUNK, 2 * HEAD), 0)
    lane = lax.broadcasted_iota(jnp.int32, (CHUNK, 2 * HEAD), 1)
    lo = lane < HEAD
    col = jnp.where(lo, lane, lane - HEAD)
    eye = (row == col).astype(F32)
    dirs = ((atf_ref, btf_ref, ktf_ref, rtf_ref, vf_ref, wcf_ref, yf_ref),
            (atb_ref, btb_ref, ktb_ref, rtb_ref, vb_ref, wcb_ref, yb_ref))
    insts = [(d, p) for d in range(2) for p in range(n_pairs)]
    lanes = lambda p: slice(p * 128, (p + 1) * 128)

    xs, ts, a_ak, a_rb, a_rk = [], [], [], [], []
    for d, p in insts:
        at_ref, bt_ref, kt_ref, rt_ref = dirs[d][:4]
        strict = (row > col) if d == 0 else (row < col)
        incl = (row >= col) if d == 0 else (row <= col)
        lhs = jnp.concatenate([at_ref[:, lanes(p)], rt_ref[:, lanes(p)]], axis=0)
        rhs = jnp.concatenate([_bd(bt_ref[:, lanes(p)], lo), _bd(kt_ref[:, lanes(p)], lo)], axis=0)
        out = _dot_nt(lhs, rhs)
        x = jnp.where(strict, out[:CHUNK, :128], 0.0)
        xs.append(x)
        ts.append(eye + x)
        a_ak.append(jnp.where(strict, out[:CHUNK, 128:], 0.0).astype(BF16))
        a_rb.append(jnp.where(incl, out[CHUNK:, :128], 0.0).astype(BF16))
        a_rk.append(jnp.where(incl, out[CHUNK:, 128:], 0.0).astype(BF16))

    n_steps = int(math.log2(CHUNK))
    for j in range(n_steps):
        for i in range(len(insts)):
            xb = xs[i].astype(BF16)
            rhs = _bd(xb, lo)
            if j == 0:
                xs[i] = _dot(xb, rhs)
            elif j < n_steps - 1:
                out = _dot(jnp.concatenate([xb, ts[i].astype(BF16)], axis=0), rhs)
                xs[i] = out[:CHUNK]
                ts[i] = ts[i] + out[CHUNK:]
            else:
                ts[i] = ts[i] + _dot(ts[i].astype(BF16), rhs)

    akv, arkv = [], []
    for i, (d, p) in enumerate(insts):
        out = _dot(jnp.concatenate([a_ak[i], a_rk[i]], axis=0), _bd(dirs[d][4][:, lanes(p)], lo))
        akv.append(out[:CHUNK].astype(BF16))
        arkv.append(out[CHUNK:])

    ps, qs = [], []
    for i, (d, p) in enumerate(insts):
        rhs = jnp.concatenate([_bd(dirs[d][0][:, lanes(p)], lo), _bd(akv[i], lo)], axis=1)
        pq = _dot(ts[i].astype(BF16), rhs)
        ps.append(pq[:, :128].astype(BF16))
        qs.append(pq[:, 128:].astype(BF16))

    rps, y0s = [], []
    for i, (d, p) in enumerate(insts):
        ry = _dot(a_rb[i], jnp.concatenate([_bd(ps[i], lo), _bd(qs[i], lo)], axis=1))
        rps.append((dirs[d][3][:, lanes(p)].astype(F32) + ry[:, :128]).astype(BF16))
        y0s.append(ry[:, 128:] + arkv[i])

    ms, ns = [], []
    for i, (d, p) in enumerate(insts):
        bt_ref, kt_ref, v_ref, wc_ref = dirs[d][1], dirs[d][2], dirs[d][4], dirs[d][5]
        wc = wc_ref[:, lanes(p)]
        b_end = (bt_ref[:, lanes(p)].astype(F32) * wc).astype(BF16)
        k_end = (kt_ref[:, lanes(p)].astype(F32) * wc).astype(BF16)
        lhs_t = jnp.concatenate([b_end, k_end], axis=0)
        v = v_ref[:, lanes(p)]
        rhs = jnp.concatenate([jnp.concatenate([ps[i], qs[i]], axis=1),
                               jnp.concatenate([jnp.zeros_like(v), v], axis=1)], axis=0)
        full = _dot_tn(lhs_t, rhs)
        ms.append((eye * wc + _undiag(full[:, :128], lo)).astype(BF16))
        ns.append(_undiag(full[:, 128:], lo))

    for i, (d, p) in enumerate(insts):
        s0 = s_ref[d, :, lanes(p)]
        out = _dot(jnp.concatenate([ms[i], rps[i]], axis=0), _bd(s0.astype(BF16), lo))
        s_ref[d, :, lanes(p)] = out[:CHUNK] + ns[i]
        dirs[d][6][:, lanes(p)] = out[CHUNK:] + y0s[i]


def _wkv(at, bt, kt, rt, v, wc, seq, n_pairs):
    T = v.shape[0]
    nc = seq // CHUNK
    lw = n_pairs * 2 * HEAD
    pos = (lambda b, g, c: b * nc + c, lambda b, g, c: b * nc + nc - 1 - c)
    specs = []
    for d in range(2):
        tok = pl.BlockSpec((None, CHUNK, lw), lambda b, g, c, d=d: (d, pos[d](b, g, c), g))
        specs += [tok] * 4 + [pl.BlockSpec((CHUNK, lw), lambda b, g, c, d=d: (pos[d](b, g, c), g)),
                              pl.BlockSpec((None, None, 1, lw), lambda b, g, c, d=d: (d, pos[d](b, g, c), 0, g))]
    return pl.pallas_call(
        functools.partial(_wkv_kernel, n_pairs),
        grid=(T // seq, D // lw, nc),
        in_specs=specs,
        out_specs=[pl.BlockSpec((CHUNK, lw), lambda b, g, c, d=d: (pos[d](b, g, c), g)) for d in range(2)],
        out_shape=[jax.ShapeDtypeStruct((T, D), F32)] * 2,
        scratch_shapes=[pltpu.VMEM((2, CHUNK, lw), F32)],
        compiler_params=pltpu.CompilerParams(dimension_semantics=("parallel", "parallel", "arbitrary"),
                                             vmem_limit_bytes=VMEM_LIMIT),
        name="wkv",
    )(at, bt, kt, rt, v, wc, at, bt, kt, rt, v, wc)


def _mix_kernel(seq, tm, yf_ref, yb_ref, bonus_ref, sg_ref, zp_ref, zpprev_ref, zpnext_ref, zg_ref, x_ref,
                lnw_ref, lnb_ref, gup_ref, wrw_ref, poolw_ref, pscale_ref, wpool_ref, bgate_ref, wout_ref,
                bd_ref, o_ref, ext_ref):
    pos0 = (pl.program_id(0) * tm) % seq
    bd = bd_ref[...]

    y = yf_ref[...] + yb_ref[...]
    mean = _seg_sum(y, bd) * (1.0 / HEAD)
    yc = y - mean
    var = _seg_sum(yc * yc, bd) * (1.0 / HEAD)
    yn = yc * lax.rsqrt(var + GN_EPS) * lnw_ref[...] + lnb_ref[...] + bonus_ref[...]
    gate = _dot(sg_ref[...].astype(BF16), gup_ref[...])
    rwkv_out = _dot((yn * gate).astype(BF16), wrw_ref[...])

    ext_ref[0:POOL_HALO, :] = jnp.where(pos0 == 0, 0.0, zpprev_ref[...])
    ext_ref[POOL_HALO:POOL_HALO + tm, :] = zp_ref[...]
    ext_ref[POOL_HALO + tm:, :] = jnp.where(pos0 + tm == seq, 0.0, zpnext_ref[...])
    pos = pos0 + lax.broadcasted_iota(jnp.int32, (tm, 1), 0)
    pooled = []
    for g, w in enumerate(POOL_WINDOWS):
        cs = slice(g * POOL_GROUP, (g + 1) * POOL_GROUP)
        acc = ext_ref[POOL_HALO - w // 2:POOL_HALO - w // 2 + tm, cs]
        for j in range(1 - w // 2, w // 2):
            acc = acc + ext_ref[POOL_HALO + j:POOL_HALO + j + tm, cs]
        cnt = jnp.minimum(pos + (w // 2 - 1), seq - 1) - jnp.maximum(pos - w // 2, 0) + 1
        pg = acc / cnt.astype(F32) - zp_ref[:, cs]
        pooled.append(_dot(pg.astype(BF16), poolw_ref[g]))
    pooled = jnp.concatenate(pooled, axis=1) * pscale_ref[...]
    pool_out = _dot(pooled.astype(BF16), wpool_ref[...])

    gates = _sigmoid(zg_ref[...] + bgate_ref[...])
    merged = gates[:, :D] * pool_out + gates[:, D:] * rwkv_out
    o_ref[...] = x_ref[...] + _dot(merged.astype(BF16), wout_ref[...])


def _mix(y_f, y_b, bonus, sg, z_pool, z_gate, x, seq, tm, p):
    T = x.shape[0]
    nb = tm // POOL_HALO
    last_blk = T // POOL_HALO - 1
    row = lambda i: (i, 0)
    consts = [p["ln_w"], p["ln_b"], p["g_up"], p["w_rwkv_br"], p["pool_w"], p["pool_scale"], p["w_pool_br"],
              p["b_gate"], p["w_out"], p["bd"]]
    return pl.pallas_call(
        functools.partial(_mix_kernel, seq, tm),
        grid=(T // tm,),
        in_specs=[pl.BlockSpec((tm, D), row), pl.BlockSpec((tm, D), row),
                  pl.BlockSpec((tm, D), row), pl.BlockSpec((tm, 128), row),
                  pl.BlockSpec((tm, POOL_WIDTH), row),
                  pl.BlockSpec((POOL_HALO, POOL_WIDTH), lambda i: (jnp.maximum(i * nb - 1, 0), 0)),
                  pl.BlockSpec((POOL_HALO, POOL_WIDTH), lambda i: (jnp.minimum((i + 1) * nb, last_blk), 0)),
                  pl.BlockSpec((tm, GATE_COLS), row), pl.BlockSpec((tm, D), row)]
                 + [_const_spec(c.shape) for c in consts],
        out_specs=pl.BlockSpec((tm, D), row),
        out_shape=jax.ShapeDtypeStruct((T, D), F32),
        scratch_shapes=[pltpu.VMEM((tm + 2 * POOL_HALO, POOL_WIDTH), F32)],
        compiler_params=_cparams(1),
        name="mix",
    )(y_f, y_b, bonus, sg, z_pool, z_pool, z_pool, z_gate, x, *consts)


def _ffn_kernel(x_ref, gffn_ref, w1_ref, w2_ref, gfin_ref, o_ref):
    x = x_ref[...]
    hn = _rms(x, gffn_ref[...]).astype(BF16)
    h = jnp.maximum(_dot(hn, w1_ref[...]), 0.0)
    x2 = x + _dot((h * h).astype(BF16), w2_ref[...])
    o_ref[...] = _rms(x2, gfin_ref[...])


def _ffn(x, g_ffn, w1, w2, g_final, tm):
    T = x.shape[0]
    row = lambda i: (i, 0)
    return pl.pallas_call(
        _ffn_kernel,
        grid=(T // tm,),
        in_specs=[pl.BlockSpec((tm, D), row), _const_spec((1, D)), _const_spec(w1.shape),
                  _const_spec(w2.shape), _const_spec((1, D))],
        out_specs=pl.BlockSpec((tm, D), row),
        out_shape=jax.ShapeDtypeStruct((T, D), F32),
        compiler_params=_cparams(1),
        name="ffn",
    )(x, g_ffn, w1, w2, g_final)


def _chunk_tri(n, upper):
    i = np.arange(n)
    same = (i[:, None] // CHUNK) == (i[None, :] // CHUNK)
    order = (i[None, :] >= i[:, None]) if upper else (i[None, :] <= i[:, None])
    return jnp.asarray(same & order, BF16)


def _prepare_params(g_mix, w_in, b_gate, mu_prev, mu_next, pool_w, pool_scale, w_pool_br, k_k, k_a, r_k,
                    w0_f, w_up_f, a0_f, a_up_f, w0_b, w_up_b, a0_b, a_up_b, g_up, ln_w, ln_b, w_rwkv_br,
                    w_out, g_ffn, w_ff1, w_ff2, g_final, tm_prep):
    row = lambda a: a.reshape(1, -1).astype(F32)
    lora = w_up_f.shape[0]
    zeros = jnp.zeros((lora, 2 * D), F32)
    seg = np.arange(SEG) // HEAD
    return {
        "g_mix": row(g_mix), "w_in": w_in.astype(BF16), "b_gate": row(b_gate),
        "mu_prev": row(mu_prev), "mu_next": row(mu_next),
        "pool_w": pool_w.astype(BF16), "pool_scale": row(pool_scale), "w_pool_br": w_pool_br.astype(BF16),
        "k_k": row(k_k), "k_a": row(k_a), "r_k": row(r_k),
        "w0": jnp.concatenate([row(w0_f), row(w0_b)], axis=1),
        "a0": jnp.concatenate([row(a0_f), row(a0_b)], axis=1),
        "w_up": jnp.concatenate([jnp.concatenate([w_up_f, w_up_b], axis=1), zeros], axis=0).astype(BF16),
        "a_up": jnp.concatenate([zeros, jnp.concatenate([a_up_f, a_up_b], axis=1)], axis=0).astype(BF16),
        "g_up": g_up.astype(BF16), "ln_w": row(ln_w), "ln_b": row(ln_b),
        "w_rwkv_br": w_rwkv_br.astype(BF16), "w_out": w_out.astype(BF16),
        "g_ffn": row(g_ffn), "w_ff1": w_ff1.astype(BF16), "w_ff2": w_ff2.astype(BF16), "g_final": row(g_final),
        "bd": jnp.asarray(seg[:, None] == seg[None, :], BF16),
        "tri_f": _chunk_tri(tm_prep, upper=False), "tri_b": _chunk_tri(tm_prep, upper=True),
    }


TM_PROJ = 256
TM_PREP = 256
WKV_PAIRS = 4
TM_MIX = 256
TM_FFN = 512


def _trunk(x, p):
    batch, seq, _ = x.shape
    xf = x.reshape(batch * seq, D)
    z_pool, z_rw, z_gate = _in_proj(xf, p["g_mix"], p["w_in"], TM_PROJ)
    at, bt, kt, rt, v, wc, bonus, sg = _prep(z_rw, seq, TM_PREP, p)
    y_f, y_b = _wkv(at, bt, kt, rt, v, wc, seq, WKV_PAIRS)
    x1 = _mix(y_f, y_b, bonus, sg, z_pool, z_gate, xf, seq, TM_MIX, p)
    out = _ffn(x1, p["g_ffn"], p["w_ff1"], p["w_ff2"], p["g_final"], TM_FFN)
    return out.reshape(batch, seq, D)


def kernel(x_prompt, x_sample, g_mix, w_in, b_gate, mu_prev, mu_next, pool_w, pool_scale, w_pool_br, k_k, k_a, r_k, w0_f, w_up_f, a0_f, a_up_f, w0_b, w_up_b, a0_b, a_up_b, g_up, ln_w, ln_b, w_rwkv_br, w_out, g_ffn, w_ff1, w_ff2, g_final):
    depth = g_mix.shape[0]
    layers = [_prepare_params(g_mix[l], w_in[l], b_gate[l], mu_prev[l], mu_next[l], pool_w[l], pool_scale[l],
                              w_pool_br[l], k_k[l], k_a[l], r_k[l], w0_f[l], w_up_f[l], a0_f[l], a_up_f[l],
                              w0_b[l], w_up_b[l], a0_b[l], a_up_b[l], g_up[l], ln_w[l], ln_b[l], w_rwkv_br[l],
                              w_out[l], g_ffn[l], w_ff1[l], w_ff2[l], g_final, TM_PREP) for l in range(depth)]
    assert depth == 1, "the final norm is fused into the last layer's ffn; only depth 1 is supported"
    return tuple(_trunk(x, layers[0]) for x in (x_prompt, x_sample))
```

```python
import functools
import math

import jax
import jax.numpy as jnp
import numpy as np
from jax import lax
from jax.experimental import pallas as pl
from jax.experimental.pallas import tpu as pltpu

D = 1024
HEAD = 64
N_HEADS = D // HEAD
POOL_WIDTH = 512
POOL_GROUP = 128
POOL_WINDOWS = (2, 4, 8, 16)
POOL_HALO = 8
LORA_COLS = 128
RW_COLS = 3 * D + 256
GATE_COLS = 2 * D
D_FF = 4 * D
RMS_EPS = 1e-6
GN_EPS = 64e-5
L2_EPS = 1e-12
CHUNK = 64
SEG = 256
VMEM_LIMIT = 56 * 1024 * 1024

F32 = jnp.float32
BF16 = jnp.bfloat16


def _dot(a, b):
    return jnp.dot(a, b, preferred_element_type=F32)


def _dot_nt(a, b):
    return lax.dot_general(a, b, (((1,), (1,)), ((), ())), preferred_element_type=F32)


def _dot_tn(a, b):
    return lax.dot_general(a, b, (((0,), (0,)), ((), ())), preferred_element_type=F32)


def _split(x):
    hi = x.astype(BF16)
    lo = (x - hi.astype(F32)).astype(BF16)
    return hi, lo


def _seg_sum(x, bd):
    outs = []
    for g in range(x.shape[1] // SEG):
        hi, lo = _split(x[:, g * SEG:(g + 1) * SEG])
        outs.append(_dot(hi, bd) + _dot(lo, bd))
    return jnp.concatenate(outs, axis=1)


def _rms(x, g):
    return x * lax.rsqrt(jnp.mean(x * x, axis=-1, keepdims=True) + RMS_EPS) * g


def _sigmoid(x):
    return 1.0 / (1.0 + jnp.exp(-x))


def _cparams(*sem):
    return pltpu.CompilerParams(dimension_semantics=sem, vmem_limit_bytes=VMEM_LIMIT)


def _const_spec(shape):
    nd = len(shape)
    return pl.BlockSpec(shape, lambda *_: (0,) * nd, pipeline_mode=pl.Buffered(1))


def _halo_specs(tm, n_rows, width):
    nb = tm // POOL_HALO
    last_blk = n_rows // POOL_HALO - 1
    return [pl.BlockSpec((POOL_HALO, width), lambda i: (jnp.maximum(i * nb - 1, 0), 0)),
            pl.BlockSpec((POOL_HALO, width), lambda i: (jnp.minimum((i + 1) * nb, last_blk), 0))]


def _in_proj_kernel(seq, tm, x_ref, xprev_ref, xnext_ref, g_ref, w_ref, mup_ref, mun_ref,
                    zp_ref, zr_ref, zg_ref):
    pos0 = (pl.program_id(0) * tm) % seq
    at_start = pos0 == 0
    at_end = pos0 + tm == seq
    rows = lax.broadcasted_iota(jnp.int32, (tm, 1), 0)
    xn = _rms(x_ref[...], g_ref[...]).astype(BF16)
    zp_ref[...] = _dot(xn, w_ref[:, 0:POOL_WIDTH])
    zg_ref[...] = _dot(xn, w_ref[:, POOL_WIDTH + RW_COLS:])
    halo = jnp.concatenate([xprev_ref[...], xnext_ref[...]], axis=0)
    hn = _rms(halo, g_ref[...]).astype(BF16)
    for c0 in range(0, RW_COLS, D):
        c1 = min(c0 + D, RW_COLS)
        w = w_ref[:, POOL_WIDTH + c0:POOL_WIDTH + c1]
        z = _dot(xn, w)
        zh = _dot(hn, w)
        prev_row = jnp.where(at_start, 0.0, zh[POOL_HALO - 1:POOL_HALO])
        next_row = jnp.where(at_end, 0.0, zh[POOL_HALO:POOL_HALO + 1])
        z_prev = jnp.where(rows == 0, prev_row, pltpu.roll(z, 1, axis=0))
        z_next = jnp.where(rows == tm - 1, next_row, pltpu.roll(z, tm - 1, axis=0))
        zs = z + mup_ref[:, c0:c1] * (z_prev - z) + mun_ref[:, c0:c1] * (z_next - z)
        zr_ref[:, c0:c1] = zs.astype(BF16)


def _in_proj(x, seq, tm, p):
    T = x.shape[0]
    row = lambda i: (i, 0)
    consts = [p["g_mix"], p["w_in"], p["mu_prev"], p["mu_next"]]
    return pl.pallas_call(
        functools.partial(_in_proj_kernel, seq, tm),
        grid=(T // tm,),
        in_specs=[pl.BlockSpec((tm, D), row)] + _halo_specs(tm, T, D) + [_const_spec(c.shape) for c in consts],
        out_specs=[pl.BlockSpec((tm, POOL_WIDTH), row), pl.BlockSpec((tm, RW_COLS), row),
                   pl.BlockSpec((tm, GATE_COLS), row)],
        out_shape=[jax.ShapeDtypeStruct((T, POOL_WIDTH), F32), jax.ShapeDtypeStruct((T, RW_COLS), BF16),
                   jax.ShapeDtypeStruct((T, GATE_COLS), F32)],
        compiler_params=_cparams("parallel"),
        name="in_proj",
    )(x, x, x, *consts)


def _bd(x, lo):
    zero = jnp.zeros_like(x)
    return jnp.concatenate([jnp.where(lo, x, zero), jnp.where(lo, zero, x)], axis=0)


def _undiag(full, lo):
    return jnp.where(lo, full[:CHUNK], full[CHUNK:])


def _wkv_kernel(n_pairs, rf_ref, kf_ref, vf_ref, lf_ref, rb_ref, kb_ref, vb_ref, lb_ref,
                kk_ref, ka_ref, rk_ref, w0f_ref, w0b_ref, a0f_ref, a0b_ref,
                wupf_ref, wupb_ref, aupf_ref, aupb_ref, bd_ref, trif_ref, trib_ref,
                yf_ref, yb_ref, bonus_ref, s_ref):
    @pl.when(pl.program_id(2) == 0)
    def _():
        s_ref[...] = jnp.zeros_like(s_ref)

    bd = bd_ref[...]
    k_k, k_a = kk_ref[...], ka_ref[...]
    y_refs = (yf_ref, yb_ref)

    def learning_rate(lora_in, a0_ref, aup_ref):
        return _sigmoid(a0_ref[...] + _dot(lora_in, aup_ref[...]))

    def chunk_operands(d, r_ref, k_ref, lora_ref, w0_ref, wup_ref, a0_ref, aup_ref, tri_ref):
        r = r_ref[...].astype(F32)
        k = k_ref[...].astype(F32)
        lora_in = lora_ref[...]
        kkr = k * k_k
        kk = kkr * jnp.minimum(lax.rsqrt(_seg_sum(kkr * kkr, bd)), 1.0 / L2_EPS)
        w_raw = w0_ref[...] + _dot(jnp.tanh(lora_in.astype(F32)).astype(BF16), wup_ref[...])
        lw = -_sigmoid(w_raw) * math.exp(-0.5)
        a = learning_rate(lora_in, a0_ref, aup_ref)
        kd = k * (1.0 + (a - 1.0) * k_a)
        hi, lo_part = _split(lw)
        tri = tri_ref[...]
        cum = _dot(tri, hi) + _dot(tri, lo_part)
        e_neg = jnp.exp(-cum)
        last = CHUNK - 1 if d == 0 else 0
        ops = dict(rt=(r * jnp.exp(cum)).astype(BF16), bt=(kk * a * e_neg).astype(BF16),
                   kt=(kd * e_neg).astype(BF16), at=(-kk * jnp.exp(cum - lw)).astype(BF16),
                   wc=jnp.exp(cum[last:last + 1, :]))
        return ops, r, k, kd

    ops_f, r_f, k_f, kd_f = chunk_operands(0, rf_ref, kf_ref, lf_ref, w0f_ref, wupf_ref, a0f_ref, aupf_ref,
                                           trif_ref)
    ops_b, _, _, _ = chunk_operands(1, rb_ref, kb_ref, lb_ref, w0b_ref, wupb_ref, a0b_ref, aupb_ref, trib_ref)
    ops_f["v"] = vf_ref[...]
    ops_b["v"] = vb_ref[...]
    ops = (ops_f, ops_b)
    kd_b_here = k_f * (1.0 + (learning_rate(lf_ref[...], a0b_ref, aupb_ref) - 1.0) * k_a)
    bonus_ref[...] = _seg_sum(r_f * (kd_f + kd_b_here) * rk_ref[...], bd) * vf_ref[...].astype(F32)

    row = lax.broadcasted_iota(jnp.int32, (CHUNK, 2 * HEAD), 0)
    lane = lax.broadcasted_iota(jnp.int32, (CHUNK, 2 * HEAD), 1)
    lo = lane < HEAD
    col = jnp.where(lo, lane, lane - HEAD)
    eye = (row == col).astype(F32)
    insts = [(d, p) for d in range(2) for p in range(n_pairs)]
    lanes = lambda p: slice(p * 128, (p + 1) * 128)
    get = lambda d, p, name: ops[d][name][:, lanes(p)]

    xs, ts, a_ak, a_rb, a_rk = [], [], [], [], []
    for d, p in insts:
        strict = (row > col) if d == 0 else (row < col)
        incl = (row >= col) if d == 0 else (row <= col)
        lhs = jnp.concatenate([get(d, p, "at"), get(d, p, "rt")], axis=0)
        rhs = jnp.concatenate([_bd(get(d, p, "bt"), lo), _bd(get(d, p, "kt"), lo)], axis=0)
        out = _dot_nt(lhs, rhs)
        x = jnp.where(strict, out[:CHUNK, :128], 0.0)
        xs.append(x)
        ts.append(eye + x)
        a_ak.append(jnp.where(strict, out[:CHUNK, 128:], 0.0).astype(BF16))
        a_rb.append(jnp.where(incl, out[CHUNK:, :128], 0.0).astype(BF16))
        a_rk.append(jnp.where(incl, out[CHUNK:, 128:], 0.0).astype(BF16))

    n_steps = int(math.log2(CHUNK))
    for j in range(n_steps):
        for i in range(len(insts)):
            xb = xs[i].astype(BF16)
            rhs = _bd(xb, lo)
            if j == 0:
                xs[i] = _dot(xb, rhs)
            elif j < n_steps - 1:
                out = _dot(jnp.concatenate([xb, ts[i].astype(BF16)], axis=0), rhs)
                xs[i] = out[:CHUNK]
                ts[i] = ts[i] + out[CHUNK:]
            else:
                ts[i] = ts[i] + _dot(ts[i].astype(BF16), rhs)

    akv, arkv = [], []
    for i, (d, p) in enumerate(insts):
        out = _dot(jnp.concatenate([a_ak[i], a_rk[i]], axis=0), _bd(get(d, p, "v"), lo))
        akv.append(out[:CHUNK].astype(BF16))
        arkv.append(out[CHUNK:])

    ps, qs = [], []
    for i, (d, p) in enumerate(insts):
        rhs = jnp.concatenate([_bd(get(d, p, "at"), lo), _bd(akv[i], lo)], axis=1)
        pq = _dot(ts[i].astype(BF16), rhs)
        ps.append(pq[:, :128].astype(BF16))
        qs.append(pq[:, 128:].astype(BF16))

    rps, y0s = [], []
    for i, (d, p) in enumerate(insts):
        ry = _dot(a_rb[i], jnp.concatenate([_bd(ps[i], lo), _bd(qs[i], lo)], axis=1))
        rps.append((get(d, p, "rt").astype(F32) + ry[:, :128]).astype(BF16))
        y0s.append(ry[:, 128:] + arkv[i])

    ms, ns = [], []
    for i, (d, p) in enumerate(insts):
        wc = get(d, p, "wc")
        b_end = (get(d, p, "bt").astype(F32) * wc).astype(BF16)
        k_end = (get(d, p, "kt").astype(F32) * wc).astype(BF16)
        lhs_t = jnp.concatenate([b_end, k_end], axis=0)
        v = get(d, p, "v")
        rhs = jnp.concatenate([jnp.concatenate([ps[i], qs[i]], axis=1),
                               jnp.concatenate([jnp.zeros_like(v), v], axis=1)], axis=0)
        full = _dot_tn(lhs_t, rhs)
        ms.append((eye * wc + _undiag(full[:, :128], lo)).astype(BF16))
        ns.append(_undiag(full[:, 128:], lo))

    for i, (d, p) in enumerate(insts):
        s0 = s_ref[d, :, lanes(p)]
        out = _dot(jnp.concatenate([ms[i], rps[i]], axis=0), _bd(s0.astype(BF16), lo))
        s_ref[d, :, lanes(p)] = out[:CHUNK] + ns[i]
        y_refs[d][:, lanes(p)] = out[CHUNK:] + y0s[i]


def _wkv(zs, seq, n_pairs, p):
    T = zs.shape[0]
    nc = seq // CHUNK
    lw = n_pairs * 2 * HEAD
    ng = D // lw
    pos = (lambda b, c: b * nc + c, lambda b, c: b * nc + nc - 1 - c)
    tok_specs = []
    for d in range(2):
        for sec in range(3):
            tok_specs.append(pl.BlockSpec((CHUNK, lw), lambda b, g, c, d=d, sec=sec: (pos[d](b, c), sec * ng + g)))
        tok_specs.append(pl.BlockSpec((CHUNK, LORA_COLS), lambda b, g, c, d=d: (pos[d](b, c), 3 * D // LORA_COLS)))
    head = pl.BlockSpec((1, lw), lambda b, g, c: (0, g))
    dir_row = lambda d: pl.BlockSpec((1, lw), lambda b, g, c: (0, d * ng + g))
    dir_lora = lambda d: pl.BlockSpec((LORA_COLS, lw), lambda b, g, c: (0, d * ng + g))
    param_specs = [head, head, head, dir_row(0), dir_row(1), dir_row(0), dir_row(1),
                   dir_lora(0), dir_lora(1), dir_lora(0), dir_lora(1),
                   _const_spec(p["bd"].shape), _const_spec(p["tri_f"].shape), _const_spec(p["tri_b"].shape)]
    params = [p["k_k"], p["k_a"], p["r_k"], p["w0"], p["w0"], p["a0"], p["a0"],
              p["w_up"], p["w_up"], p["a_up"], p["a_up"], p["bd"], p["tri_f"], p["tri_b"]]
    out_spec = lambda d: pl.BlockSpec((CHUNK, lw), lambda b, g, c: (pos[d](b, c), g))
    return pl.pallas_call(
        functools.partial(_wkv_kernel, n_pairs),
        grid=(T // seq, ng, nc),
        in_specs=tok_specs + param_specs,
        out_specs=[out_spec(0), out_spec(1), out_spec(0)],
        out_shape=[jax.ShapeDtypeStruct((T, D), F32)] * 3,
        scratch_shapes=[pltpu.VMEM((2, CHUNK, lw), F32)],
        compiler_params=_cparams("parallel", "parallel", "arbitrary"),
        name="wkv",
    )(*([zs] * 8), *params)


def _mix_kernel(seq, tm, yf_ref, yb_ref, bonus_ref, zgl_ref, zp_ref, zpprev_ref, zpnext_ref, zg_ref, x_ref,
                lnw_ref, lnb_ref, gup_ref, wrw_ref, poolw_ref, pscale_ref, wpool_ref, bgate_ref, wout_ref,
                bd_ref, o_ref, ext_ref):
    pos0 = (pl.program_id(0) * tm) % seq
    bd = bd_ref[...]

    y = yf_ref[...] + yb_ref[...]
    mean = _seg_sum(y, bd) * (1.0 / HEAD)
    yc = y - mean
    var = _seg_sum(yc * yc, bd) * (1.0 / HEAD)
    yn = yc * lax.rsqrt(var + GN_EPS) * lnw_ref[...] + lnb_ref[...] + bonus_ref[...]
    gate = _dot(_sigmoid(zgl_ref[...].astype(F32)).astype(BF16), gup_ref[...])
    rwkv_out = _dot((yn * gate).astype(BF16), wrw_ref[...])

    ext_ref[0:POOL_HALO, :] = jnp.where(pos0 == 0, 0.0, zpprev_ref[...])
    ext_ref[POOL_HALO:POOL_HALO + tm, :] = zp_ref[...]
    ext_ref[POOL_HALO + tm:, :] = jnp.where(pos0 + tm == seq, 0.0, zpnext_ref[...])
    pos = pos0 + lax.broadcasted_iota(jnp.int32, (tm, 1), 0)
    pooled = []
    for g, w in enumerate(POOL_WINDOWS):
        cs = slice(g * POOL_GROUP, (g + 1) * POOL_GROUP)
        acc = ext_ref[POOL_HALO - w // 2:POOL_HALO - w // 2 + tm, cs]
        for j in range(1 - w // 2, w // 2):
            acc = acc + ext_ref[POOL_HALO + j:POOL_HALO + j + tm, cs]
        cnt = jnp.minimum(pos + (w // 2 - 1), seq - 1) - jnp.maximum(pos - w // 2, 0) + 1
        pg = acc / cnt.astype(F32) - zp_ref[:, cs]
        pooled.append(_dot(pg.astype(BF16), poolw_ref[g]))
    pooled = jnp.concatenate(pooled, axis=1) * pscale_ref[...]
    pool_out = _dot(pooled.astype(BF16), wpool_ref[...])

    gates = _sigmoid(zg_ref[...] + bgate_ref[...])
    merged = gates[:, :D] * pool_out + gates[:, D:] * rwkv_out
    o_ref[...] = x_ref[...] + _dot(merged.astype(BF16), wout_ref[...])


def _mix(y_f, y_b, bonus, zs, z_pool, z_gate, x, seq, tm, p):
    T = x.shape[0]
    row = lambda i: (i, 0)
    consts = [p["ln_w"], p["ln_b"], p["g_up"], p["w_rwkv_br"], p["pool_w"], p["pool_scale"], p["w_pool_br"],
              p["b_gate"], p["w_out"], p["bd"]]
    gate_lora_blk = (3 * D + LORA_COLS) // 128
    return pl.pallas_call(
        functools.partial(_mix_kernel, seq, tm),
        grid=(T // tm,),
        in_specs=[pl.BlockSpec((tm, D), row), pl.BlockSpec((tm, D), row), pl.BlockSpec((tm, D), row),
                  pl.BlockSpec((tm, 128), lambda i: (i, gate_lora_blk)),
                  pl.BlockSpec((tm, POOL_WIDTH), row)] + _halo_specs(tm, T, POOL_WIDTH)
                 + [pl.BlockSpec((tm, GATE_COLS), row), pl.BlockSpec((tm, D), row)]
                 + [_const_spec(c.shape) for c in consts],
        out_specs=pl.BlockSpec((tm, D), row),
        out_shape=jax.ShapeDtypeStruct((T, D), F32),
        scratch_shapes=[pltpu.VMEM((tm + 2 * POOL_HALO, POOL_WIDTH), F32)],
        compiler_params=_cparams("parallel"),
        name="mix",
    )(y_f, y_b, bonus, zs, z_pool, z_pool, z_pool, z_gate, x, *consts)


def _ffn_kernel(x_ref, gffn_ref, w1_ref, w2_ref, gfin_ref, o_ref):
    x = x_ref[...]
    hn = _rms(x, gffn_ref[...]).astype(BF16)
    h = jnp.maximum(_dot(hn, w1_ref[...]), 0.0)
    x2 = x + _dot((h * h).astype(BF16), w2_ref[...])
    o_ref[...] = _rms(x2, gfin_ref[...])


def _ffn(x, g_ffn, w1, w2, g_final, tm):
    T = x.shape[0]
    row = lambda i: (i, 0)
    return pl.pallas_call(
        _ffn_kernel,
        grid=(T // tm,),
        in_specs=[pl.BlockSpec((tm, D), row), _const_spec((1, D)), _const_spec(w1.shape),
                  _const_spec(w2.shape), _const_spec((1, D))],
        out_specs=pl.BlockSpec((tm, D), row),
        out_shape=jax.ShapeDtypeStruct((T, D), F32),
        compiler_params=_cparams("parallel"),
        name="ffn",
    )(x, g_ffn, w1, w2, g_final)


def _prepare_params(g_mix, w_in, b_gate, mu_prev, mu_next, pool_w, pool_scale, w_pool_br, k_k, k_a, r_k,
                    w0_f, w_up_f, a0_f, a_up_f, w0_b, w_up_b, a0_b, a_up_b, g_up, ln_w, ln_b, w_rwkv_br,
                    w_out, g_ffn, w_ff1, w_ff2, g_final):
    row = lambda a: a.reshape(1, -1).astype(F32)
    lora = w_up_f.shape[0]
    zeros = jnp.zeros((lora, 2 * D), F32)
    seg = np.arange(SEG) // HEAD
    t = np.arange(CHUNK)
    return {
        "g_mix": row(g_mix), "w_in": w_in.astype(BF16), "b_gate": row(b_gate),
        "mu_prev": row(mu_prev), "mu_next": row(mu_next),
        "pool_w": pool_w.astype(BF16), "pool_scale": row(pool_scale), "w_pool_br": w_pool_br.astype(BF16),
        "k_k": row(k_k), "k_a": row(k_a), "r_k": row(r_k),
        "w0": jnp.concatenate([row(w0_f), row(w0_b)], axis=1),
        "a0": jnp.concatenate([row(a0_f), row(a0_b)], axis=1),
        "w_up": jnp.concatenate([jnp.concatenate([w_up_f, w_up_b], axis=1), zeros], axis=0).astype(BF16),
        "a_up": jnp.concatenate([zeros, jnp.concatenate([a_up_f, a_up_b], axis=1)], axis=0).astype(BF16),
        "g_up": g_up.astype(BF16), "ln_w": row(ln_w), "ln_b": row(ln_b),
        "w_rwkv_br": w_rwkv_br.astype(BF16), "w_out": w_out.astype(BF16),
        "g_ffn": row(g_ffn), "w_ff1": w_ff1.astype(BF16), "w_ff2": w_ff2.astype(BF16), "g_final": row(g_final),
        "bd": jnp.asarray(seg[:, None] == seg[None, :], BF16),
        "tri_f": jnp.asarray(t[None, :] <= t[:, None], BF16), "tri_b": jnp.asarray(t[None, :] >= t[:, None], BF16),
    }


TM_PROJ = 256
WKV_PAIRS = 4
TM_MIX = 256
TM_FFN = 512


def _trunk(x, p):
    batch, seq, _ = x.shape
    xf = x.reshape(batch * seq, D)
    z_pool, zs, z_gate = _in_proj(xf, seq, TM_PROJ, p)
    y_f, y_b, bonus = _wkv(zs, seq, WKV_PAIRS, p)
    x1 = _mix(y_f, y_b, bonus, zs, z_pool, z_gate, xf, seq, TM_MIX, p)
    out = _ffn(x1, p["g_ffn"], p["w_ff1"], p["w_ff2"], p["g_final"], TM_FFN)
    return out.reshape(batch, seq, D)


def kernel(x_prompt, x_sample, g_mix, w_in, b_gate, mu_prev, mu_next, pool_w, pool_scale, w_pool_br, k_k, k_a, r_k, w0_f, w_up_f, a0_f, a_up_f, w0_b, w_up_b, a0_b, a_up_b, g_up, ln_w, ln_b, w_rwkv_br, w_out, g_ffn, w_ff1, w_ff2, g_final):
    depth = g_mix.shape[0]
    layers = [_prepare_params(g_mix[l], w_in[l], b_gate[l], mu_prev[l], mu_next[l], pool_w[l], pool_scale[l],
                              w_pool_br[l], k_k[l], k_a[l], r_k[l], w0_f[l], w_up_f[l], a0_f[l], a_up_f[l],
                              w0_b[l], w_up_b[l], a0_b[l], a_up_b[l], g_up[l], ln_w[l], ln_b[l], w_rwkv_br[l],
                              w_out[l], g_ffn[l], w_ff1[l], w_ff2[l], g_final) for l in range(depth)]
    assert depth == 1, "the final norm is fused into the last layer's ffn; only depth 1 is supported"
    return tuple(_trunk(x, layers[0]) for x in (x_prompt, x_sample))
```

```python
import functools
import math

import jax
import jax.numpy as jnp
import numpy as np
from jax import lax
from jax.experimental import pallas as pl
from jax.experimental.pallas import tpu as pltpu

D = 1024
HEAD = 64
N_HEADS = D // HEAD
POOL_WIDTH = 512
POOL_GROUP = 128
POOL_WINDOWS = (2, 4, 8, 16)
POOL_HALO = 8
LORA_COLS = 128
RW_COLS = 3 * D + 256
GATE_COLS = 2 * D
D_FF = 4 * D
RMS_EPS = 1e-6
GN_EPS = 64e-5
L2_EPS = 1e-12
CHUNK = 64
SEG = 256
VMEM_LIMIT = 56 * 1024 * 1024

F32 = jnp.float32
BF16 = jnp.bfloat16


def _dot(a, b):
    return jnp.dot(a, b, preferred_element_type=F32)


def _dot_nt(a, b):
    return lax.dot_general(a, b, (((1,), (1,)), ((), ())), preferred_element_type=F32)


def _dot_tn(a, b):
    return lax.dot_general(a, b, (((0,), (0,)), ((), ())), preferred_element_type=F32)


def _split(x):
    hi = x.astype(BF16)
    lo = (x - hi.astype(F32)).astype(BF16)
    return hi, lo


def _seg_sum(x, bd):
    outs = []
    for g in range(x.shape[1] // SEG):
        hi, lo = _split(x[:, g * SEG:(g + 1) * SEG])
        outs.append(_dot(hi, bd) + _dot(lo, bd))
    return jnp.concatenate(outs, axis=1)


def _rms(x, g):
    return x * lax.rsqrt(jnp.mean(x * x, axis=-1, keepdims=True) + RMS_EPS) * g


def _sigmoid(x):
    return 1.0 / (1.0 + jnp.exp(-x))


def _cparams(*sem):
    return pltpu.CompilerParams(dimension_semantics=sem, vmem_limit_bytes=VMEM_LIMIT)


def _const_spec(shape):
    nd = len(shape)
    return pl.BlockSpec(shape, lambda *_: (0,) * nd, pipeline_mode=pl.Buffered(1))


def _halo_specs(tm, n_rows, width):
    nb = tm // POOL_HALO
    last_blk = n_rows // POOL_HALO - 1
    return [pl.BlockSpec((POOL_HALO, width), lambda i: (jnp.maximum(i * nb - 1, 0), 0)),
            pl.BlockSpec((POOL_HALO, width), lambda i: (jnp.minimum((i + 1) * nb, last_blk), 0))]


def _in_proj_kernel(seq, tm, x_ref, xprev_ref, xnext_ref, g_ref, w_ref, mup_ref, mun_ref,
                    zp_ref, zr_ref, zg_ref):
    pos0 = (pl.program_id(0) * tm) % seq
    at_start = pos0 == 0
    at_end = pos0 + tm == seq
    rows = lax.broadcasted_iota(jnp.int32, (tm, 1), 0)
    xn = _rms(x_ref[...], g_ref[...]).astype(BF16)
    zp_ref[...] = _dot(xn, w_ref[:, 0:POOL_WIDTH])
    zg_ref[...] = _dot(xn, w_ref[:, POOL_WIDTH + RW_COLS:])
    halo = jnp.concatenate([xprev_ref[...], xnext_ref[...]], axis=0)
    hn = _rms(halo, g_ref[...]).astype(BF16)
    for c0 in range(0, RW_COLS, D):
        c1 = min(c0 + D, RW_COLS)
        w = w_ref[:, POOL_WIDTH + c0:POOL_WIDTH + c1]
        z = _dot(xn, w)
        zh = _dot(hn, w)
        prev_row = jnp.where(at_start, 0.0, zh[POOL_HALO - 1:POOL_HALO])
        next_row = jnp.where(at_end, 0.0, zh[POOL_HALO:POOL_HALO + 1])
        z_prev = jnp.where(rows == 0, prev_row, pltpu.roll(z, 1, axis=0))
        z_next = jnp.where(rows == tm - 1, next_row, pltpu.roll(z, tm - 1, axis=0))
        zs = z + mup_ref[:, c0:c1] * (z_prev - z) + mun_ref[:, c0:c1] * (z_next - z)
        zr_ref[:, c0:c1] = zs.astype(BF16)


def _in_proj(x, seq, tm, p):
    T = x.shape[0]
    row = lambda i: (i, 0)
    consts = [p["g_mix"], p["w_in"], p["mu_prev"], p["mu_next"]]
    return pl.pallas_call(
        functools.partial(_in_proj_kernel, seq, tm),
        grid=(T // tm,),
        in_specs=[pl.BlockSpec((tm, D), row)] + _halo_specs(tm, T, D) + [_const_spec(c.shape) for c in consts],
        out_specs=[pl.BlockSpec((tm, POOL_WIDTH), row), pl.BlockSpec((tm, RW_COLS), row),
                   pl.BlockSpec((tm, GATE_COLS), row)],
        out_shape=[jax.ShapeDtypeStruct((T, POOL_WIDTH), F32), jax.ShapeDtypeStruct((T, RW_COLS), BF16),
                   jax.ShapeDtypeStruct((T, GATE_COLS), F32)],
        compiler_params=_cparams("parallel"),
        name="in_proj",
    )(x, x, x, *consts)


def _bd(x, lo):
    zero = jnp.zeros_like(x)
    return jnp.concatenate([jnp.where(lo, x, zero), jnp.where(lo, zero, x)], axis=0)


def _undiag(full, lo):
    return jnp.where(lo, full[:CHUNK], full[CHUNK:])


def _wkv_kernel(n_pairs, rf_ref, kf_ref, vf_ref, lf_ref, rb_ref, kb_ref, vb_ref, lb_ref,
                kk_ref, ka_ref, rk_ref, w0f_ref, w0b_ref, a0f_ref, a0b_ref,
                wupf_ref, wupb_ref, aupf_ref, aupb_ref, bd_ref, trif_ref, trib_ref,
                yf_ref, yb_ref, bonus_ref, s_ref):
    @pl.when(pl.program_id(2) == 0)
    def _():
        s_ref[...] = jnp.zeros_like(s_ref)

    bd = bd_ref[...]
    k_k, k_a = kk_ref[...], ka_ref[...]
    y_refs = (yf_ref, yb_ref)

    def learning_rate(lora_in, a0_ref, aup_ref):
        return _sigmoid(a0_ref[...] + _dot(lora_in, aup_ref[...]))

    def chunk_operands(d, r_ref, k_ref, lora_ref, w0_ref, wup_ref, a0_ref, aup_ref, tri_ref):
        r = r_ref[...].astype(F32)
        k = k_ref[...].astype(F32)
        lora_in = lora_ref[...]
        kkr = k * k_k
        kk = kkr * jnp.minimum(lax.rsqrt(_seg_sum(kkr * kkr, bd)), 1.0 / L2_EPS)
        w_raw = w0_ref[...] + _dot(jnp.tanh(lora_in.astype(F32)).astype(BF16), wup_ref[...])
        lw = -_sigmoid(w_raw) * math.exp(-0.5)
        a = learning_rate(lora_in, a0_ref, aup_ref)
        kd = k * (1.0 + (a - 1.0) * k_a)
        hi, lo_part = _split(lw)
        tri = tri_ref[...]
        cum = _dot(tri, hi) + _dot(tri, lo_part)
        e_neg = jnp.exp(-cum)
        last = CHUNK - 1 if d == 0 else 0
        ops = dict(rt=(r * jnp.exp(cum)).astype(BF16), bt=(kk * a * e_neg).astype(BF16),
                   kt=(kd * e_neg).astype(BF16), at=(-kk * jnp.exp(cum - lw)).astype(BF16),
                   wc=jnp.exp(cum[last:last + 1, :]))
        return ops, r, k, kd

    ops_f, r_f, k_f, kd_f = chunk_operands(0, rf_ref, kf_ref, lf_ref, w0f_ref, wupf_ref, a0f_ref, aupf_ref,
                                           trif_ref)
    ops_b, _, _, _ = chunk_operands(1, rb_ref, kb_ref, lb_ref, w0b_ref, wupb_ref, a0b_ref, aupb_ref, trib_ref)
    ops_f["v"] = vf_ref[...]
    ops_b["v"] = vb_ref[...]
    ops = (ops_f, ops_b)
    kd_b_here = k_f * (1.0 + (learning_rate(lf_ref[...], a0b_ref, aupb_ref) - 1.0) * k_a)
    bonus_ref[...] = _seg_sum(r_f * (kd_f + kd_b_here) * rk_ref[...], bd) * vf_ref[...].astype(F32)

    row = lax.broadcasted_iota(jnp.int32, (CHUNK, 2 * HEAD), 0)
    lane = lax.broadcasted_iota(jnp.int32, (CHUNK, 2 * HEAD), 1)
    lo = lane < HEAD
    col = jnp.where(lo, lane, lane - HEAD)
    eye = (row == col).astype(F32)
    insts = [(d, p) for d in range(2) for p in range(n_pairs)]
    lanes = lambda p: slice(p * 128, (p + 1) * 128)
    get = lambda d, p, name: ops[d][name][:, lanes(p)]

    xs, ts, a_ak, a_rb, a_rk = [], [], [], [], []
    for d, p in insts:
        strict = (row > col) if d == 0 else (row < col)
        incl = (row >= col) if d == 0 else (row <= col)
        lhs = jnp.concatenate([get(d, p, "at"), get(d, p, "rt")], axis=0)
        rhs = jnp.concatenate([_bd(get(d, p, "bt"), lo), _bd(get(d, p, "kt"), lo)], axis=0)
        out = _dot_nt(lhs, rhs)
        x = jnp.where(strict, out[:CHUNK, :128], 0.0)
        xs.append(x)
        ts.append(eye + x)
        a_ak.append(jnp.where(strict, out[:CHUNK, 128:], 0.0).astype(BF16))
        a_rb.append(jnp.where(incl, out[CHUNK:, :128], 0.0).astype(BF16))
        a_rk.append(jnp.where(incl, out[CHUNK:, 128:], 0.0).astype(BF16))

    n_steps = int(math.log2(CHUNK))
    for j in range(n_steps):
        for i in range(len(insts)):
            xb = xs[i].astype(BF16)
            rhs = _bd(xb, lo)
            if j == 0:
                xs[i] = _dot(xb, rhs)
            elif j < n_steps - 1:
                out = _dot(jnp.concatenate([xb, ts[i].astype(BF16)], axis=0), rhs)
                xs[i] = out[:CHUNK]
                ts[i] = ts[i] + out[CHUNK:]
            else:
                ts[i] = ts[i] + _dot(ts[i].astype(BF16), rhs)

    akv, arkv = [], []
    for i, (d, p) in enumerate(insts):
        out = _dot(jnp.concatenate([a_ak[i], a_rk[i]], axis=0), _bd(get(d, p, "v"), lo))
        akv.append(out[:CHUNK].astype(BF16))
        arkv.append(out[CHUNK:])

    ps, qs = [], []
    for i, (d, p) in enumerate(insts):
        rhs = jnp.concatenate([_bd(get(d, p, "at"), lo), _bd(akv[i], lo)], axis=1)
        pq = _dot(ts[i].astype(BF16), rhs)
        ps.append(pq[:, :128].astype(BF16))
        qs.append(pq[:, 128:].astype(BF16))

    rps, y0s = [], []
    for i, (d, p) in enumerate(insts):
        ry = _dot(a_rb[i], jnp.concatenate([_bd(ps[i], lo), _bd(qs[i], lo)], axis=1))
        rps.append((get(d, p, "rt").astype(F32) + ry[:, :128]).astype(BF16))
        y0s.append(ry[:, 128:] + arkv[i])

    ms, ns = [], []
    for i, (d, p) in enumerate(insts):
        wc = get(d, p, "wc")
        b_end = (get(d, p, "bt").astype(F32) * wc).astype(BF16)
        k_end = (get(d, p, "kt").astype(F32) * wc).astype(BF16)
        lhs_t = jnp.concatenate([b_end, k_end], axis=0)
        v = get(d, p, "v")
        rhs = jnp.concatenate([jnp.concatenate([ps[i], qs[i]], axis=1),
                               jnp.concatenate([jnp.zeros_like(v), v], axis=1)], axis=0)
        full = _dot_tn(lhs_t, rhs)
        ms.append((eye * wc + _undiag(full[:, :128], lo)).astype(BF16))
        ns.append(_undiag(full[:, 128:], lo))

    for i, (d, p) in enumerate(insts):
        s0 = s_ref[d, :, lanes(p)]
        out = _dot(jnp.concatenate([ms[i], rps[i]], axis=0), _bd(s0.astype(BF16), lo))
        s_ref[d, :, lanes(p)] = out[:CHUNK] + ns[i]
        y_refs[d][:, lanes(p)] = out[CHUNK:] + y0s[i]


def _wkv(zs, seq, n_pairs, p):
    T = zs.shape[0]
    nc = seq // CHUNK
    lw = n_pairs * 2 * HEAD
    ng = D // lw
    pos = (lambda b, c: b * nc + c, lambda b, c: b * nc + nc - 1 - c)
    tok_specs = []
    for d in range(2):
        for sec in range(3):
            tok_specs.append(pl.BlockSpec((CHUNK, lw), lambda b, g, c, d=d, sec=sec: (pos[d](b, c), sec * ng + g)))
        tok_specs.append(pl.BlockSpec((CHUNK, LORA_COLS), lambda b, g, c, d=d: (pos[d](b, c), 3 * D // LORA_COLS)))
    head = pl.BlockSpec((1, lw), lambda b, g, c: (0, g))
    dir_row = lambda d: pl.BlockSpec((1, lw), lambda b, g, c: (0, d * ng + g))
    dir_lora = lambda d: pl.BlockSpec((LORA_COLS, lw), lambda b, g, c: (0, d * ng + g))
    param_specs = [head, head, head, dir_row(0), dir_row(1), dir_row(0), dir_row(1),
                   dir_lora(0), dir_lora(1), dir_lora(0), dir_lora(1),
                   _const_spec(p["bd"].shape), _const_spec(p["tri_f"].shape), _const_spec(p["tri_b"].shape)]
    params = [p["k_k"], p["k_a"], p["r_k"], p["w0"], p["w0"], p["a0"], p["a0"],
              p["w_up"], p["w_up"], p["a_up"], p["a_up"], p["bd"], p["tri_f"], p["tri_b"]]
    out_spec = lambda d: pl.BlockSpec((CHUNK, lw), lambda b, g, c: (pos[d](b, c), g))
    return pl.pallas_call(
        functools.partial(_wkv_kernel, n_pairs),
        grid=(T // seq, ng, nc),
        in_specs=tok_specs + param_specs,
        out_specs=[out_spec(0), out_spec(1), out_spec(0)],
        out_shape=[jax.ShapeDtypeStruct((T, D), F32)] * 3,
        scratch_shapes=[pltpu.VMEM((2, CHUNK, lw), F32)],
        compiler_params=_cparams("parallel", "parallel", "arbitrary"),
        name="wkv",
    )(*([zs] * 8), *params)


def _mix_kernel(seq, tm, yf_ref, yb_ref, bonus_ref, zgl_ref, zp_ref, zpprev_ref, zpnext_ref, zg_ref, x_ref,
                lnw_ref, lnb_ref, gup_ref, wrw_ref, poolw_ref, pscale_ref, wpool_ref, bgate_ref, wout_ref,
                bd_ref, o_ref, ext_ref):
    pos0 = (pl.program_id(0) * tm) % seq
    bd = bd_ref[...]

    y = yf_ref[...] + yb_ref[...]
    mean = _seg_sum(y, bd) * (1.0 / HEAD)
    yc = y - mean
    var = _seg_sum(yc * yc, bd) * (1.0 / HEAD)
    yn = yc * lax.rsqrt(var + GN_EPS) * lnw_ref[...] + lnb_ref[...] + bonus_ref[...]
    gate = _dot(_sigmoid(zgl_ref[...].astype(F32)).astype(BF16), gup_ref[...])
    rwkv_out = _dot((yn * gate).astype(BF16), wrw_ref[...])

    ext_ref[0:POOL_HALO, :] = jnp.where(pos0 == 0, 0.0, zpprev_ref[...])
    ext_ref[POOL_HALO:POOL_HALO + tm, :] = zp_ref[...]
    ext_ref[POOL_HALO + tm:, :] = jnp.where(pos0 + tm == seq, 0.0, zpnext_ref[...])
    pos = pos0 + lax.broadcasted_iota(jnp.int32, (tm, 1), 0)
    pooled = []
    for g, w in enumerate(POOL_WINDOWS):
        cs = slice(g * POOL_GROUP, (g + 1) * POOL_GROUP)
        acc = ext_ref[POOL_HALO - w // 2:POOL_HALO - w // 2 + tm, cs]
        for j in range(1 - w // 2, w // 2):
            acc = acc + ext_ref[POOL_HALO + j:POOL_HALO + j + tm, cs]
        cnt = jnp.minimum(pos + (w // 2 - 1), seq - 1) - jnp.maximum(pos - w // 2, 0) + 1
        pg = acc / cnt.astype(F32) - zp_ref[:, cs]
        pooled.append(_dot(pg.astype(BF16), poolw_ref[g]))
    pooled = jnp.concatenate(pooled, axis=1) * pscale_ref[...]
    pool_out = _dot(pooled.astype(BF16), wpool_ref[...])

    gates = _sigmoid(zg_ref[...] + bgate_ref[...])
    merged = gates[:, :D] * pool_out + gates[:, D:] * rwkv_out
    o_ref[...] = x_ref[...] + _dot(merged.astype(BF16), wout_ref[...])


def _mix(y_f, y_b, bonus, zs, z_pool, z_gate, x, seq, tm, p):
    T = x.shape[0]
    row = lambda i: (i, 0)
    consts = [p["ln_w"], p["ln_b"], p["g_up"], p["w_rwkv_br"], p["pool_w"], p["pool_scale"], p["w_pool_br"],
              p["b_gate"], p["w_out"], p["bd"]]
    gate_lora_blk = (3 * D + LORA_COLS) // 128
    return pl.pallas_call(
        functools.partial(_mix_kernel, seq, tm),
        grid=(T // tm,),
        in_specs=[pl.BlockSpec((tm, D), row), pl.BlockSpec((tm, D), row), pl.BlockSpec((tm, D), row),
                  pl.BlockSpec((tm, 128), lambda i: (i, gate_lora_blk)),
                  pl.BlockSpec((tm, POOL_WIDTH), row)] + _halo_specs(tm, T, POOL_WIDTH)
                 + [pl.BlockSpec((tm, GATE_COLS), row), pl.BlockSpec((tm, D), row)]
                 + [_const_spec(c.shape) for c in consts],
        out_specs=pl.BlockSpec((tm, D), row),
        out_shape=jax.ShapeDtypeStruct((T, D), F32),
        scratch_shapes=[pltpu.VMEM((tm + 2 * POOL_HALO, POOL_WIDTH), F32)],
        compiler_params=_cparams("parallel"),
        name="mix",
    )(y_f, y_b, bonus, zs, z_pool, z_pool, z_pool, z_gate, x, *consts)


def _ffn_kernel(x_ref, gffn_ref, w1_ref, w2_ref, gfin_ref, o_ref):
    x = x_ref[...]
    hn = _rms(x, gffn_ref[...]).astype(BF16)
    h = jnp.maximum(_dot(hn, w1_ref[...]), 0.0)
    x2 = x + _dot((h * h).astype(BF16), w2_ref[...])
    o_ref[...] = _rms(x2, gfin_ref[...])


def _ffn(x, g_ffn, w1, w2, g_final, tm):
    T = x.shape[0]
    row = lambda i: (i, 0)
    return pl.pallas_call(
        _ffn_kernel,
        grid=(T // tm,),
        in_specs=[pl.BlockSpec((tm, D), row), _const_spec((1, D)), _const_spec(w1.shape),
                  _const_spec(w2.shape), _const_spec((1, D))],
        out_specs=pl.BlockSpec((tm, D), row),
        out_shape=jax.ShapeDtypeStruct((T, D), F32),
        compiler_params=_cparams("parallel"),
        name="ffn",
    )(x, g_ffn, w1, w2, g_final)


def _prepare_params(g_mix, w_in, b_gate, mu_prev, mu_next, pool_w, pool_scale, w_pool_br, k_k, k_a, r_k,
                    w0_f, w_up_f, a0_f, a_up_f, w0_b, w_up_b, a0_b, a_up_b, g_up, ln_w, ln_b, w_rwkv_br,
                    w_out, g_ffn, w_ff1, w_ff2, g_final):
    row = lambda a: a.reshape(1, -1).astype(F32)
    lora = w_up_f.shape[0]
    zeros = jnp.zeros((lora, 2 * D), F32)
    seg = np.arange(SEG) // HEAD
    t = np.arange(CHUNK)
    return {
        "g_mix": row(g_mix), "w_in": w_in.astype(BF16), "b_gate": row(b_gate),
        "mu_prev": row(mu_prev), "mu_next": row(mu_next),
        "pool_w": pool_w.astype(BF16), "pool_scale": row(pool_scale), "w_pool_br": w_pool_br.astype(BF16),
        "k_k": row(k_k), "k_a": row(k_a), "r_k": row(r_k),
        "w0": jnp.concatenate([row(w0_f), row(w0_b)], axis=1),
        "a0": jnp.concatenate([row(a0_f), row(a0_b)], axis=1),
        "w_up": jnp.concatenate([jnp.concatenate([w_up_f, w_up_b], axis=1), zeros], axis=0).astype(BF16),
        "a_up": jnp.concatenate([zeros, jnp.concatenate([a_up_f, a_up_b], axis=1)], axis=0).astype(BF16),
        "g_up": g_up.astype(BF16), "ln_w": row(ln_w), "ln_b": row(ln_b),
        "w_rwkv_br": w_rwkv_br.astype(BF16), "w_out": w_out.astype(BF16),
        "g_ffn": row(g_ffn), "w_ff1": w_ff1.astype(BF16), "w_ff2": w_ff2.astype(BF16), "g_final": row(g_final),
        "bd": jnp.asarray(seg[:, None] == seg[None, :], BF16),
        "tri_f": jnp.asarray(t[None, :] <= t[:, None], BF16), "tri_b": jnp.asarray(t[None, :] >= t[:, None], BF16),
    }


TM_PROJ = 256
WKV_PAIRS = 8
TM_MIX = 256
TM_FFN = 512


def _trunk(x, p):
    batch, seq, _ = x.shape
    xf = x.reshape(batch * seq, D)
    z_pool, zs, z_gate = _in_proj(xf, seq, TM_PROJ, p)
    y_f, y_b, bonus = _wkv(zs, seq, WKV_PAIRS, p)
    x1 = _mix(y_f, y_b, bonus, zs, z_pool, z_gate, xf, seq, TM_MIX, p)
    out = _ffn(x1, p["g_ffn"], p["w_ff1"], p["w_ff2"], p["g_final"], TM_FFN)
    return out.reshape(batch, seq, D)


def kernel(x_prompt, x_sample, g_mix, w_in, b_gate, mu_prev, mu_next, pool_w, pool_scale, w_pool_br, k_k, k_a, r_k, w0_f, w_up_f, a0_f, a_up_f, w0_b, w_up_b, a0_b, a_up_b, g_up, ln_w, ln_b, w_rwkv_br, w_out, g_ffn, w_ff1, w_ff2, g_final):
    depth = g_mix.shape[0]
    layers = [_prepare_params(g_mix[l], w_in[l], b_gate[l], mu_prev[l], mu_next[l], pool_w[l], pool_scale[l],
                              w_pool_br[l], k_k[l], k_a[l], r_k[l], w0_f[l], w_up_f[l], a0_f[l], a_up_f[l],
                              w0_b[l], w_up_b[l], a0_b[l], a_up_b[l], g_up[l], ln_w[l], ln_b[l], w_rwkv_br[l],
                              w_out[l], g_ffn[l], w_ff1[l], w_ff2[l], g_final) for l in range(depth)]
    assert depth == 1, "the final norm is fused into the last layer's ffn; only depth 1 is supported"
    return tuple(_trunk(x, layers[0]) for x in (x_prompt, x_sample))
```

```python
import functools
import math

import jax
import jax.numpy as jnp
import numpy as np
from jax import lax
from jax.experimental import pallas as pl
from jax.experimental.pallas import tpu as pltpu

D = 1024
HEAD = 64
N_HEADS = D // HEAD
POOL_WIDTH = 512
POOL_GROUP = 128
POOL_WINDOWS = (2, 4, 8, 16)
POOL_HALO = 8
LORA_COLS = 128
RW_COLS = 3 * D + 256
GATE_COLS = 2 * D
D_FF = 4 * D
RMS_EPS = 1e-6
GN_EPS = 64e-5
L2_EPS = 1e-12
CHUNK = 64
SEG = 256
VMEM_LIMIT = 56 * 1024 * 1024

F32 = jnp.float32
BF16 = jnp.bfloat16


def _dot(a, b):
    return jnp.dot(a, b, preferred_element_type=F32)


def _dot_nt(a, b):
    return lax.dot_general(a, b, (((1,), (1,)), ((), ())), preferred_element_type=F32)


def _dot_tn(a, b):
    return lax.dot_general(a, b, (((0,), (0,)), ((), ())), preferred_element_type=F32)


def _split(x):
    hi = x.astype(BF16)
    lo = (x - hi.astype(F32)).astype(BF16)
    return hi, lo


def _seg_sum(x, bd):
    rows, n_groups = x.shape[0], x.shape[1] // SEG
    xb = x.astype(BF16)
    stacked = jnp.concatenate([xb[:, g * SEG:(g + 1) * SEG] for g in range(n_groups)], axis=0)
    out = _dot(stacked, bd)
    return jnp.concatenate([out[g * rows:(g + 1) * rows] for g in range(n_groups)], axis=1)


def _rms(x, g):
    return x * lax.rsqrt(jnp.mean(x * x, axis=-1, keepdims=True) + RMS_EPS) * g


def _sigmoid(x):
    return 1.0 / (1.0 + jnp.exp(-x))


def _cparams(*sem):
    return pltpu.CompilerParams(dimension_semantics=sem, vmem_limit_bytes=VMEM_LIMIT)


def _const_spec(shape):
    nd = len(shape)
    return pl.BlockSpec(shape, lambda *_: (0,) * nd, pipeline_mode=pl.Buffered(1))


def _halo_specs(tm, n_rows, width):
    nb = tm // POOL_HALO
    last_blk = n_rows // POOL_HALO - 1
    return [pl.BlockSpec((POOL_HALO, width), lambda i: (jnp.maximum(i * nb - 1, 0), 0)),
            pl.BlockSpec((POOL_HALO, width), lambda i: (jnp.minimum((i + 1) * nb, last_blk), 0))]


def _in_proj_kernel(seq, tm, x_ref, xprev_ref, xnext_ref, g_ref, w_ref, mup_ref, mun_ref,
                    zp_ref, zr_ref, zg_ref):
    pos0 = (pl.program_id(0) * tm) % seq
    at_start = pos0 == 0
    at_end = pos0 + tm == seq
    rows = lax.broadcasted_iota(jnp.int32, (tm, 1), 0)
    xn = _rms(x_ref[...], g_ref[...]).astype(BF16)
    zp_ref[...] = _dot(xn, w_ref[:, 0:POOL_WIDTH])
    zg_ref[...] = _dot(xn, w_ref[:, POOL_WIDTH + RW_COLS:])
    halo = jnp.concatenate([xprev_ref[...], xnext_ref[...]], axis=0)
    hn = _rms(halo, g_ref[...]).astype(BF16)
    for c0 in range(0, RW_COLS, D):
        c1 = min(c0 + D, RW_COLS)
        w = w_ref[:, POOL_WIDTH + c0:POOL_WIDTH + c1]
        z = _dot(xn, w)
        zh = _dot(hn, w)
        prev_row = jnp.where(at_start, 0.0, zh[POOL_HALO - 1:POOL_HALO])
        next_row = jnp.where(at_end, 0.0, zh[POOL_HALO:POOL_HALO + 1])
        z_prev = jnp.where(rows == 0, prev_row, pltpu.roll(z, 1, axis=0))
        z_next = jnp.where(rows == tm - 1, next_row, pltpu.roll(z, tm - 1, axis=0))
        zs = z + mup_ref[:, c0:c1] * (z_prev - z) + mun_ref[:, c0:c1] * (z_next - z)
        zr_ref[:, c0:c1] = zs.astype(BF16)


def _in_proj(x, seq, tm, p):
    T = x.shape[0]
    row = lambda i: (i, 0)
    consts = [p["g_mix"], p["w_in"], p["mu_prev"], p["mu_next"]]
    return pl.pallas_call(
        functools.partial(_in_proj_kernel, seq, tm),
        grid=(T // tm,),
        in_specs=[pl.BlockSpec((tm, D), row)] + _halo_specs(tm, T, D) + [_const_spec(c.shape) for c in consts],
        out_specs=[pl.BlockSpec((tm, POOL_WIDTH), row), pl.BlockSpec((tm, RW_COLS), row),
                   pl.BlockSpec((tm, GATE_COLS), row)],
        out_shape=[jax.ShapeDtypeStruct((T, POOL_WIDTH), F32), jax.ShapeDtypeStruct((T, RW_COLS), BF16),
                   jax.ShapeDtypeStruct((T, GATE_COLS), F32)],
        compiler_params=_cparams("parallel"),
        name="in_proj",
    )(x, x, x, *consts)


def _bd(x, lo):
    zero = jnp.zeros_like(x)
    return jnp.concatenate([jnp.where(lo, x, zero), jnp.where(lo, zero, x)], axis=0)


def _undiag(full, lo):
    return jnp.where(lo, full[:CHUNK], full[CHUNK:])


def _wkv_kernel(zf_ref, zb_ref, kk_ref, ka_ref, rk_ref, w0_ref, a0_ref, wup_ref, aup_ref,
                bd_ref, trif_ref, trib_ref, yf_ref, yb_ref, bonus_ref, s_ref):
    @pl.when(pl.program_id(1) == 0)
    def _():
        s_ref[...] = jnp.zeros_like(s_ref)

    bd = bd_ref[...]
    k_k, k_a = kk_ref[...], ka_ref[...]
    y_refs = (yf_ref, yb_ref)
    n_pairs = D // (2 * HEAD)
    dir_cols = lambda d: slice(d * D, (d + 1) * D)
    lora_cols = slice(3 * D, 3 * D + LORA_COLS)

    def learning_rate(lora_in, d):
        return _sigmoid(a0_ref[:, dir_cols(d)] + _dot(lora_in, aup_ref[:, dir_cols(d)]))

    def chunk_operands(d, z_ref, tri_ref):
        r = z_ref[:, 0:D].astype(F32)
        k = z_ref[:, D:2 * D].astype(F32)
        lora_in = z_ref[:, lora_cols]
        kkr = k * k_k
        kk = kkr * jnp.minimum(lax.rsqrt(_seg_sum(kkr * kkr, bd)), 1.0 / L2_EPS)
        w_raw = w0_ref[:, dir_cols(d)] + _dot(jnp.tanh(lora_in.astype(F32)).astype(BF16),
                                              wup_ref[:, dir_cols(d)])
        lw = -_sigmoid(w_raw) * math.exp(-0.5)
        a = learning_rate(lora_in, d)
        kd = k * (1.0 + (a - 1.0) * k_a)
        hi, lo_part = _split(lw)
        tri = tri_ref[...]
        cum = _dot(tri, hi) + _dot(tri, lo_part)
        e_neg = jnp.exp(-cum)
        last = CHUNK - 1 if d == 0 else 0
        ops = dict(rt=(r * jnp.exp(cum)).astype(BF16), bt=(kk * a * e_neg).astype(BF16),
                   kt=(kd * e_neg).astype(BF16), at=(-kk * jnp.exp(cum - lw)).astype(BF16),
                   wc=jnp.exp(cum[last:last + 1, :]))
        return ops, r, k, kd

    ops_f, r_f, k_f, kd_f = chunk_operands(0, zf_ref, trif_ref)
    ops_b, _, _, _ = chunk_operands(1, zb_ref, trib_ref)
    ops_f["v"] = zf_ref[:, 2 * D:3 * D]
    ops_b["v"] = zb_ref[:, 2 * D:3 * D]
    ops = (ops_f, ops_b)
    kd_b_here = k_f * (1.0 + (learning_rate(zf_ref[:, lora_cols], 1) - 1.0) * k_a)
    bonus_ref[...] = _seg_sum(r_f * (kd_f + kd_b_here) * rk_ref[...], bd) * ops_f["v"].astype(F32)

    row = lax.broadcasted_iota(jnp.int32, (CHUNK, 2 * HEAD), 0)
    lane = lax.broadcasted_iota(jnp.int32, (CHUNK, 2 * HEAD), 1)
    lo = lane < HEAD
    col = jnp.where(lo, lane, lane - HEAD)
    eye = (row == col).astype(F32)
    insts = [(d, p) for d in range(2) for p in range(n_pairs)]
    lanes = lambda p: slice(p * 128, (p + 1) * 128)
    get = lambda d, p, name: ops[d][name][:, lanes(p)]

    xs, ts, a_ak, a_rb, a_rk = [], [], [], [], []
    for d, p in insts:
        strict = (row > col) if d == 0 else (row < col)
        incl = (row >= col) if d == 0 else (row <= col)
        lhs = jnp.concatenate([get(d, p, "at"), get(d, p, "rt")], axis=0)
        rhs = jnp.concatenate([_bd(get(d, p, "bt"), lo), _bd(get(d, p, "kt"), lo)], axis=0)
        out = _dot_nt(lhs, rhs)
        x = jnp.where(strict, out[:CHUNK, :128], 0.0)
        xs.append(x)
        ts.append(eye + x)
        a_ak.append(jnp.where(strict, out[:CHUNK, 128:], 0.0).astype(BF16))
        a_rb.append(jnp.where(incl, out[CHUNK:, :128], 0.0).astype(BF16))
        a_rk.append(jnp.where(incl, out[CHUNK:, 128:], 0.0).astype(BF16))

    n_steps = int(math.log2(CHUNK))
    for j in range(n_steps):
        for i in range(len(insts)):
            xb = xs[i].astype(BF16)
            rhs = _bd(xb, lo)
            if j == 0:
                xs[i] = _dot(xb, rhs)
            elif j < n_steps - 1:
                out = _dot(jnp.concatenate([xb, ts[i].astype(BF16)], axis=0), rhs)
                xs[i] = out[:CHUNK]
                ts[i] = ts[i] + out[CHUNK:]
            else:
                ts[i] = ts[i] + _dot(ts[i].astype(BF16), rhs)

    akv, arkv = [], []
    for i, (d, p) in enumerate(insts):
        out = _dot(jnp.concatenate([a_ak[i], a_rk[i]], axis=0), _bd(get(d, p, "v"), lo))
        akv.append(out[:CHUNK].astype(BF16))
        arkv.append(out[CHUNK:])

    ps, qs = [], []
    for i, (d, p) in enumerate(insts):
        rhs = jnp.concatenate([_bd(get(d, p, "at"), lo), _bd(akv[i], lo)], axis=1)
        pq = _dot(ts[i].astype(BF16), rhs)
        ps.append(pq[:, :128].astype(BF16))
        qs.append(pq[:, 128:].astype(BF16))

    rps, y0s = [], []
    for i, (d, p) in enumerate(insts):
        ry = _dot(a_rb[i], jnp.concatenate([_bd(ps[i], lo), _bd(qs[i], lo)], axis=1))
        rps.append((get(d, p, "rt").astype(F32) + ry[:, :128]).astype(BF16))
        y0s.append(ry[:, 128:] + arkv[i])

    ms, ns = [], []
    for i, (d, p) in enumerate(insts):
        wc = get(d, p, "wc")
        b_end = (get(d, p, "bt").astype(F32) * wc).astype(BF16)
        k_end = (get(d, p, "kt").astype(F32) * wc).astype(BF16)
        lhs_t = jnp.concatenate([b_end, k_end], axis=0)
        v = get(d, p, "v")
        rhs = jnp.concatenate([jnp.concatenate([ps[i], qs[i]], axis=1),
                               jnp.concatenate([jnp.zeros_like(v), v], axis=1)], axis=0)
        full = _dot_tn(lhs_t, rhs)
        ms.append((eye * wc + _undiag(full[:, :128], lo)).astype(BF16))
        ns.append(_undiag(full[:, 128:], lo))

    for i, (d, p) in enumerate(insts):
        s0 = s_ref[d, :, lanes(p)]
        out = _dot(jnp.concatenate([ms[i], rps[i]], axis=0), _bd(s0.astype(BF16), lo))
        s_ref[d, :, lanes(p)] = out[:CHUNK] + ns[i]
        y_refs[d][:, lanes(p)] = out[CHUNK:] + y0s[i]


def _wkv(zs, seq, p):
    T = zs.shape[0]
    nc = seq // CHUNK
    pos = (lambda b, c: (b * nc + c, 0), lambda b, c: (b * nc + nc - 1 - c, 0))
    params = [p["k_k"], p["k_a"], p["r_k"], p["w0"], p["a0"], p["w_up"], p["a_up"], p["bd"], p["tri_f"],
              p["tri_b"]]
    return pl.pallas_call(
        _wkv_kernel,
        grid=(T // seq, nc),
        in_specs=[pl.BlockSpec((CHUNK, RW_COLS), pos[0]), pl.BlockSpec((CHUNK, RW_COLS), pos[1])]
                 + [_const_spec(c.shape) for c in params],
        out_specs=[pl.BlockSpec((CHUNK, D), pos[0]), pl.BlockSpec((CHUNK, D), pos[1]),
                   pl.BlockSpec((CHUNK, D), pos[0])],
        out_shape=[jax.ShapeDtypeStruct((T, D), F32)] * 3,
        scratch_shapes=[pltpu.VMEM((2, CHUNK, D), F32)],
        compiler_params=_cparams("parallel", "arbitrary"),
        name="wkv",
    )(zs, zs, *params)


def _mix_kernel(seq, tm, yf_ref, yb_ref, bonus_ref, zgl_ref, zp_ref, zpprev_ref, zpnext_ref, zg_ref, x_ref,
                lnw_ref, lnb_ref, gup_ref, wrw_ref, poolw_ref, pscale_ref, wpool_ref, bgate_ref, wout_ref,
                bd_ref, o_ref, ext_ref):
    pos0 = (pl.program_id(0) * tm) % seq
    bd = bd_ref[...]

    y = yf_ref[...] + yb_ref[...]
    mean = _seg_sum(y, bd) * (1.0 / HEAD)
    yc = y - mean
    var = _seg_sum(yc * yc, bd) * (1.0 / HEAD)
    yn = yc * lax.rsqrt(var + GN_EPS) * lnw_ref[...] + lnb_ref[...] + bonus_ref[...]
    gate = _dot(_sigmoid(zgl_ref[...].astype(F32)).astype(BF16), gup_ref[...])
    rwkv_out = _dot((yn * gate).astype(BF16), wrw_ref[...])

    ext_ref[0:POOL_HALO, :] = jnp.where(pos0 == 0, 0.0, zpprev_ref[...])
    ext_ref[POOL_HALO:POOL_HALO + tm, :] = zp_ref[...]
    ext_ref[POOL_HALO + tm:, :] = jnp.where(pos0 + tm == seq, 0.0, zpnext_ref[...])
    pos = pos0 + lax.broadcasted_iota(jnp.int32, (tm, 1), 0)
    pooled = []
    for g, w in enumerate(POOL_WINDOWS):
        cs = slice(g * POOL_GROUP, (g + 1) * POOL_GROUP)
        acc = ext_ref[POOL_HALO - w // 2:POOL_HALO - w // 2 + tm, cs]
        for j in range(1 - w // 2, w // 2):
            acc = acc + ext_ref[POOL_HALO + j:POOL_HALO + j + tm, cs]
        cnt = jnp.minimum(pos + (w // 2 - 1), seq - 1) - jnp.maximum(pos - w // 2, 0) + 1
        pg = acc / cnt.astype(F32) - zp_ref[:, cs]
        pooled.append(_dot(pg.astype(BF16), poolw_ref[g]))
    pooled = jnp.concatenate(pooled, axis=1) * pscale_ref[...]
    pool_out = _dot(pooled.astype(BF16), wpool_ref[...])

    gates = _sigmoid(zg_ref[...] + bgate_ref[...])
    merged = gates[:, :D] * pool_out + gates[:, D:] * rwkv_out
    o_ref[...] = x_ref[...] + _dot(merged.astype(BF16), wout_ref[...])


def _mix(y_f, y_b, bonus, zs, z_pool, z_gate, x, seq, tm, p):
    T = x.shape[0]
    row = lambda i: (i, 0)
    consts = [p["ln_w"], p["ln_b"], p["g_up"], p["w_rwkv_br"], p["pool_w"], p["pool_scale"], p["w_pool_br"],
              p["b_gate"], p["w_out"], p["bd"]]
    gate_lora_blk = (3 * D + LORA_COLS) // 128
    return pl.pallas_call(
        functools.partial(_mix_kernel, seq, tm),
        grid=(T // tm,),
        in_specs=[pl.BlockSpec((tm, D), row), pl.BlockSpec((tm, D), row), pl.BlockSpec((tm, D), row),
                  pl.BlockSpec((tm, 128), lambda i: (i, gate_lora_blk)),
                  pl.BlockSpec((tm, POOL_WIDTH), row)] + _halo_specs(tm, T, POOL_WIDTH)
                 + [pl.BlockSpec((tm, GATE_COLS), row), pl.BlockSpec((tm, D), row)]
                 + [_const_spec(c.shape) for c in consts],
        out_specs=pl.BlockSpec((tm, D), row),
        out_shape=jax.ShapeDtypeStruct((T, D), F32),
        scratch_shapes=[pltpu.VMEM((tm + 2 * POOL_HALO, POOL_WIDTH), F32)],
        compiler_params=_cparams("parallel"),
        name="mix",
    )(y_f, y_b, bonus, zs, z_pool, z_pool, z_pool, z_gate, x, *consts)


def _ffn_kernel(x_ref, gffn_ref, w1_ref, w2_ref, gfin_ref, o_ref):
    x = x_ref[...]
    hn = _rms(x, gffn_ref[...]).astype(BF16)
    h = jnp.maximum(_dot(hn, w1_ref[...]), 0.0)
    x2 = x + _dot((h * h).astype(BF16), w2_ref[...])
    o_ref[...] = _rms(x2, gfin_ref[...])


def _ffn(x, g_ffn, w1, w2, g_final, tm):
    T = x.shape[0]
    row = lambda i: (i, 0)
    return pl.pallas_call(
        _ffn_kernel,
        grid=(T // tm,),
        in_specs=[pl.BlockSpec((tm, D), row), _const_spec((1, D)), _const_spec(w1.shape),
                  _const_spec(w2.shape), _const_spec((1, D))],
        out_specs=pl.BlockSpec((tm, D), row),
        out_shape=jax.ShapeDtypeStruct((T, D), F32),
        compiler_params=_cparams("parallel"),
        name="ffn",
    )(x, g_ffn, w1, w2, g_final)


def _prepare_params(g_mix, w_in, b_gate, mu_prev, mu_next, pool_w, pool_scale, w_pool_br, k_k, k_a, r_k,
                    w0_f, w_up_f, a0_f, a_up_f, w0_b, w_up_b, a0_b, a_up_b, g_up, ln_w, ln_b, w_rwkv_br,
                    w_out, g_ffn, w_ff1, w_ff2, g_final):
    row = lambda a: a.reshape(1, -1).astype(F32)
    lora = w_up_f.shape[0]
    zeros = jnp.zeros((lora, 2 * D), F32)
    seg = np.arange(SEG) // HEAD
    t = np.arange(CHUNK)
    return {
        "g_mix": row(g_mix), "w_in": w_in.astype(BF16), "b_gate": row(b_gate),
        "mu_prev": row(mu_prev), "mu_next": row(mu_next),
        "pool_w": pool_w.astype(BF16), "pool_scale": row(pool_scale), "w_pool_br": w_pool_br.astype(BF16),
        "k_k": row(k_k), "k_a": row(k_a), "r_k": row(r_k),
        "w0": jnp.concatenate([row(w0_f), row(w0_b)], axis=1),
        "a0": jnp.concatenate([row(a0_f), row(a0_b)], axis=1),
        "w_up": jnp.concatenate([jnp.concatenate([w_up_f, w_up_b], axis=1), zeros], axis=0).astype(BF16),
        "a_up": jnp.concatenate([zeros, jnp.concatenate([a_up_f, a_up_b], axis=1)], axis=0).astype(BF16),
        "g_up": g_up.astype(BF16), "ln_w": row(ln_w), "ln_b": row(ln_b),
        "w_rwkv_br": w_rwkv_br.astype(BF16), "w_out": w_out.astype(BF16),
        "g_ffn": row(g_ffn), "w_ff1": w_ff1.astype(BF16), "w_ff2": w_ff2.astype(BF16), "g_final": row(g_final),
        "bd": jnp.asarray(seg[:, None] == seg[None, :], BF16),
        "tri_f": jnp.asarray(t[None, :] <= t[:, None], BF16), "tri_b": jnp.asarray(t[None, :] >= t[:, None], BF16),
    }


TM_PROJ = 256
TM_MIX = 256
TM_FFN = 512


def _trunk(x, p):
    batch, seq, _ = x.shape
    xf = x.reshape(batch * seq, D)
    z_pool, zs, z_gate = _in_proj(xf, seq, TM_PROJ, p)
    y_f, y_b, bonus = _wkv(zs, seq, p)
    x1 = _mix(y_f, y_b, bonus, zs, z_pool, z_gate, xf, seq, TM_MIX, p)
    out = _ffn(x1, p["g_ffn"], p["w_ff1"], p["w_ff2"], p["g_final"], TM_FFN)
    return out.reshape(batch, seq, D)


def kernel(x_prompt, x_sample, g_mix, w_in, b_gate, mu_prev, mu_next, pool_w, pool_scale, w_pool_br, k_k, k_a, r_k, w0_f, w_up_f, a0_f, a_up_f, w0_b, w_up_b, a0_b, a_up_b, g_up, ln_w, ln_b, w_rwkv_br, w_out, g_ffn, w_ff1, w_ff2, g_final):
    depth = g_mix.shape[0]
    layers = [_prepare_params(g_mix[l], w_in[l], b_gate[l], mu_prev[l], mu_next[l], pool_w[l], pool_scale[l],
                              w_pool_br[l], k_k[l], k_a[l], r_k[l], w0_f[l], w_up_f[l], a0_f[l], a_up_f[l],
                              w0_b[l], w_up_b[l], a0_b[l], a_up_b[l], g_up[l], ln_w[l], ln_b[l], w_rwkv_br[l],
                              w_out[l], g_ffn[l], w_ff1[l], w_ff2[l], g_final) for l in range(depth)]
    assert depth == 1, "the final norm is fused into the last layer's ffn; only depth 1 is supported"
    return tuple(_trunk(x, layers[0]) for x in (x_prompt, x_sample))
```

```python
import functools
import math

import jax
import jax.numpy as jnp
import numpy as np
from jax import lax
from jax.experimental import pallas as pl
from jax.experimental.pallas import tpu as pltpu

D = 1024
HEAD = 64
N_HEADS = D // HEAD
POOL_WIDTH = 512
POOL_GROUP = 128
POOL_WINDOWS = (2, 4, 8, 16)
POOL_HALO = 8
LORA_COLS = 128
RW_COLS = 3 * D + 256
GATE_COLS = 2 * D
D_FF = 4 * D
RMS_EPS = 1e-6
GN_EPS = 64e-5
L2_EPS = 1e-12
CHUNK = 64
SEG = 256
VMEM_LIMIT = 56 * 1024 * 1024

F32 = jnp.float32
BF16 = jnp.bfloat16


def _dot(a, b):
    return jnp.dot(a, b, preferred_element_type=F32)


def _dot_nt(a, b):
    return lax.dot_general(a, b, (((1,), (1,)), ((), ())), preferred_element_type=F32)


def _dot_tn(a, b):
    return lax.dot_general(a, b, (((0,), (0,)), ((), ())), preferred_element_type=F32)


def _split(x):
    hi = x.astype(BF16)
    lo = (x - hi.astype(F32)).astype(BF16)
    return hi, lo


def _seg_sum(x, bd):
    rows, n_groups = x.shape[0], x.shape[1] // SEG
    xb = x.astype(BF16)
    stacked = jnp.concatenate([xb[:, g * SEG:(g + 1) * SEG] for g in range(n_groups)], axis=0)
    out = _dot(stacked, bd)
    return jnp.concatenate([out[g * rows:(g + 1) * rows] for g in range(n_groups)], axis=1)


def _rms(x, g):
    return x * lax.rsqrt(jnp.mean(x * x, axis=-1, keepdims=True) + RMS_EPS) * g


def _sigmoid(x):
    return 1.0 / (1.0 + jnp.exp(-x))


def _cparams(*sem):
    return pltpu.CompilerParams(dimension_semantics=sem, vmem_limit_bytes=VMEM_LIMIT)


def _const_spec(shape):
    nd = len(shape)
    return pl.BlockSpec(shape, lambda *_: (0,) * nd, pipeline_mode=pl.Buffered(1))


def _halo_specs(tm, n_rows, width):
    nb = tm // POOL_HALO
    last_blk = n_rows // POOL_HALO - 1
    return [pl.BlockSpec((POOL_HALO, width), lambda i: (jnp.maximum(i * nb - 1, 0), 0)),
            pl.BlockSpec((POOL_HALO, width), lambda i: (jnp.minimum((i + 1) * nb, last_blk), 0))]


def _in_proj_kernel(seq, tm, x_ref, xprev_ref, xnext_ref, g_ref, w_ref, mup_ref, mun_ref,
                    zp_ref, zr_ref, zg_ref):
    pos0 = (pl.program_id(0) * tm) % seq
    at_start = pos0 == 0
    at_end = pos0 + tm == seq
    rows = lax.broadcasted_iota(jnp.int32, (tm, 1), 0)
    xn = _rms(x_ref[...], g_ref[...]).astype(BF16)
    zp_ref[...] = _dot(xn, w_ref[:, 0:POOL_WIDTH])
    zg_ref[...] = _dot(xn, w_ref[:, POOL_WIDTH + RW_COLS:])
    halo = jnp.concatenate([xprev_ref[...], xnext_ref[...]], axis=0)
    hn = _rms(halo, g_ref[...]).astype(BF16)
    for c0 in range(0, RW_COLS, D):
        c1 = min(c0 + D, RW_COLS)
        w = w_ref[:, POOL_WIDTH + c0:POOL_WIDTH + c1]
        z = _dot(xn, w)
        zh = _dot(hn, w)
        prev_row = jnp.where(at_start, 0.0, zh[POOL_HALO - 1:POOL_HALO])
        next_row = jnp.where(at_end, 0.0, zh[POOL_HALO:POOL_HALO + 1])
        z_prev = jnp.where(rows == 0, prev_row, pltpu.roll(z, 1, axis=0))
        z_next = jnp.where(rows == tm - 1, next_row, pltpu.roll(z, tm - 1, axis=0))
        zs = z + mup_ref[:, c0:c1] * (z_prev - z) + mun_ref[:, c0:c1] * (z_next - z)
        zr_ref[:, c0:c1] = zs.astype(BF16)


def _in_proj(x, seq, tm, p):
    T = x.shape[0]
    row = lambda i: (i, 0)
    consts = [p["g_mix"], p["w_in"], p["mu_prev"], p["mu_next"]]
    return pl.pallas_call(
        functools.partial(_in_proj_kernel, seq, tm),
        grid=(T // tm,),
        in_specs=[pl.BlockSpec((tm, D), row)] + _halo_specs(tm, T, D) + [_const_spec(c.shape) for c in consts],
        out_specs=[pl.BlockSpec((tm, POOL_WIDTH), row), pl.BlockSpec((tm, RW_COLS), row),
                   pl.BlockSpec((tm, GATE_COLS), row)],
        out_shape=[jax.ShapeDtypeStruct((T, POOL_WIDTH), F32), jax.ShapeDtypeStruct((T, RW_COLS), BF16),
                   jax.ShapeDtypeStruct((T, GATE_COLS), F32)],
        compiler_params=_cparams("parallel"),
        name="in_proj",
    )(x, x, x, *consts)


def _bd(x, lo):
    zero = jnp.zeros_like(x)
    return jnp.concatenate([jnp.where(lo, x, zero), jnp.where(lo, zero, x)], axis=0)


def _undiag(full, lo):
    return jnp.where(lo, full[:CHUNK], full[CHUNK:])


def _wkv_kernel(n_chunks, zf_ref, zb_ref, kk_ref, ka_ref, rk_ref, w0_ref, a0_ref, wup_ref, aup_ref,
                bd_ref, trif_ref, trib_ref, yf_ref, yb_ref, bonus_ref, s_ref):
    @pl.when(pl.program_id(1) == 0)
    def _():
        s_ref[...] = jnp.zeros_like(s_ref)

    bd = bd_ref[...]
    k_k, k_a = kk_ref[...], ka_ref[...]
    y_refs = (yf_ref, yb_ref)
    n_pairs = D // (2 * HEAD)
    dir_cols = lambda d: slice(d * D, (d + 1) * D)
    lora_cols = slice(3 * D, 3 * D + LORA_COLS)

    def learning_rate(lora_in, d):
        return _sigmoid(a0_ref[:, dir_cols(d)] + _dot(lora_in, aup_ref[:, dir_cols(d)]))

    def chunk_operands(d, z_ref, tri_ref):
        r = z_ref[:, 0:D].astype(F32)
        k = z_ref[:, D:2 * D].astype(F32)
        lora_in = z_ref[:, lora_cols]
        kkr = k * k_k
        kk = kkr * jnp.minimum(lax.rsqrt(_seg_sum(kkr * kkr, bd)), 1.0 / L2_EPS)
        w_raw = w0_ref[:, dir_cols(d)] + _dot(jnp.tanh(lora_in.astype(F32)).astype(BF16),
                                              wup_ref[:, dir_cols(d)])
        lw = -_sigmoid(w_raw) * math.exp(-0.5)
        a = learning_rate(lora_in, d)
        kd = k * (1.0 + (a - 1.0) * k_a)
        tri = tri_ref[...]
        cum = _dot(jnp.concatenate([tri, tri], axis=1), jnp.concatenate(_split(lw), axis=0))
        e_neg = jnp.exp(-cum)
        last = CHUNK - 1 if d == 0 else 0
        ops = dict(rt=(r * jnp.exp(cum)).astype(BF16), bt=(kk * a * e_neg).astype(BF16),
                   kt=(kd * e_neg).astype(BF16), at=(-kk * jnp.exp(cum - lw)).astype(BF16),
                   wc=[jnp.exp(cum[c * CHUNK + last:c * CHUNK + last + 1, :]) for c in range(n_chunks)])
        return ops, r, k, kd

    ops_f, r_f, k_f, kd_f = chunk_operands(0, zf_ref, trif_ref)
    ops_b, _, _, _ = chunk_operands(1, zb_ref, trib_ref)
    ops_f["v"] = zf_ref[:, 2 * D:3 * D]
    ops_b["v"] = zb_ref[:, 2 * D:3 * D]
    ops = (ops_f, ops_b)
    kd_b_here = k_f * (1.0 + (learning_rate(zf_ref[:, lora_cols], 1) - 1.0) * k_a)
    bonus_ref[...] = _seg_sum(r_f * (kd_f + kd_b_here) * rk_ref[...], bd) * ops_f["v"].astype(F32)

    row = lax.broadcasted_iota(jnp.int32, (CHUNK, 2 * HEAD), 0)
    lane = lax.broadcasted_iota(jnp.int32, (CHUNK, 2 * HEAD), 1)
    lo = lane < HEAD
    col = jnp.where(lo, lane, lane - HEAD)
    eye = (row == col).astype(F32)
    insts = [(d, c, p) for d in range(2) for c in range(n_chunks) for p in range(n_pairs)]
    lanes = lambda p: slice(p * 128, (p + 1) * 128)
    rows_of = lambda c: slice(c * CHUNK, (c + 1) * CHUNK)
    get = lambda d, c, p, name: ops[d][name][rows_of(c), lanes(p)]

    xs, ts, a_ak, a_rb, a_rk = [], [], [], [], []
    for d, c, p in insts:
        strict = (row > col) if d == 0 else (row < col)
        incl = (row >= col) if d == 0 else (row <= col)
        lhs = jnp.concatenate([get(d, c, p, "at"), get(d, c, p, "rt")], axis=0)
        rhs = jnp.concatenate([_bd(get(d, c, p, "bt"), lo), _bd(get(d, c, p, "kt"), lo)], axis=0)
        out = _dot_nt(lhs, rhs)
        x = jnp.where(strict, out[:CHUNK, :128], 0.0)
        xs.append(x)
        ts.append(eye + x)
        a_ak.append(jnp.where(strict, out[:CHUNK, 128:], 0.0).astype(BF16))
        a_rb.append(jnp.where(incl, out[CHUNK:, :128], 0.0).astype(BF16))
        a_rk.append(jnp.where(incl, out[CHUNK:, 128:], 0.0).astype(BF16))

    n_steps = int(math.log2(CHUNK))
    for j in range(n_steps):
        for i in range(len(insts)):
            xb = xs[i].astype(BF16)
            rhs = _bd(xb, lo)
            if j == 0:
                xs[i] = _dot(xb, rhs)
            elif j < n_steps - 1:
                out = _dot(jnp.concatenate([xb, ts[i].astype(BF16)], axis=0), rhs)
                xs[i] = out[:CHUNK]
                ts[i] = ts[i] + out[CHUNK:]
            else:
                ts[i] = ts[i] + _dot(ts[i].astype(BF16), rhs)

    akv, arkv = [], []
    for i, (d, c, p) in enumerate(insts):
        out = _dot(jnp.concatenate([a_ak[i], a_rk[i]], axis=0), _bd(get(d, c, p, "v"), lo))
        akv.append(out[:CHUNK].astype(BF16))
        arkv.append(out[CHUNK:])

    ps, qs = [], []
    for i, (d, c, p) in enumerate(insts):
        rhs = jnp.concatenate([_bd(get(d, c, p, "at"), lo), _bd(akv[i], lo)], axis=1)
        pq = _dot(ts[i].astype(BF16), rhs)
        ps.append(pq[:, :128].astype(BF16))
        qs.append(pq[:, 128:].astype(BF16))

    rps, y0s = [], []
    for i, (d, c, p) in enumerate(insts):
        ry = _dot(a_rb[i], jnp.concatenate([_bd(ps[i], lo), _bd(qs[i], lo)], axis=1))
        rps.append((get(d, c, p, "rt").astype(F32) + ry[:, :128]).astype(BF16))
        y0s.append(ry[:, 128:] + arkv[i])

    ms, ns = [], []
    for i, (d, c, p) in enumerate(insts):
        wc = ops[d]["wc"][c][:, lanes(p)]
        b_end = (get(d, c, p, "bt").astype(F32) * wc).astype(BF16)
        k_end = (get(d, c, p, "kt").astype(F32) * wc).astype(BF16)
        lhs_t = jnp.concatenate([b_end, k_end], axis=0)
        v = get(d, c, p, "v")
        rhs = jnp.concatenate([jnp.concatenate([ps[i], qs[i]], axis=1),
                               jnp.concatenate([jnp.zeros_like(v), v], axis=1)], axis=0)
        full = _dot_tn(lhs_t, rhs)
        ms.append((eye * wc + _undiag(full[:, :128], lo)).astype(BF16))
        ns.append(_undiag(full[:, 128:], lo))

    for step in range(n_chunks):
        for i, (d, c, p) in enumerate(insts):
            if c != (step if d == 0 else n_chunks - 1 - step):
                continue
            s0 = s_ref[d, :, lanes(p)]
            out = _dot(jnp.concatenate([ms[i], rps[i]], axis=0), _bd(s0.astype(BF16), lo))
            s_ref[d, :, lanes(p)] = out[:CHUNK] + ns[i]
            y_refs[d][rows_of(c), lanes(p)] = out[CHUNK:] + y0s[i]


def _wkv(zs, seq, n_chunks, p):
    T = zs.shape[0]
    tm = n_chunks * CHUNK
    nc = seq // tm
    pos = (lambda b, c: (b * nc + c, 0), lambda b, c: (b * nc + nc - 1 - c, 0))
    params = [p["k_k"], p["k_a"], p["r_k"], p["w0"], p["a0"], p["w_up"], p["a_up"], p["bd"], p["tri_f"],
              p["tri_b"]]
    return pl.pallas_call(
        functools.partial(_wkv_kernel, n_chunks),
        grid=(T // seq, nc),
        in_specs=[pl.BlockSpec((tm, RW_COLS), pos[0]), pl.BlockSpec((tm, RW_COLS), pos[1])]
                 + [_const_spec(c.shape) for c in params],
        out_specs=[pl.BlockSpec((tm, D), pos[0]), pl.BlockSpec((tm, D), pos[1]),
                   pl.BlockSpec((tm, D), pos[0])],
        out_shape=[jax.ShapeDtypeStruct((T, D), F32)] * 3,
        scratch_shapes=[pltpu.VMEM((2, CHUNK, D), F32)],
        compiler_params=_cparams("parallel", "arbitrary"),
        name="wkv",
    )(zs, zs, *params)


def _mix_kernel(seq, tm, yf_ref, yb_ref, bonus_ref, zgl_ref, zp_ref, zpprev_ref, zpnext_ref, zg_ref, x_ref,
                lnw_ref, lnb_ref, gup_ref, wrw_ref, poolw_ref, pscale_ref, wpool_ref, bgate_ref, wout_ref,
                bd_ref, o_ref, ext_ref):
    pos0 = (pl.program_id(0) * tm) % seq
    bd = bd_ref[...]

    y = yf_ref[...] + yb_ref[...]
    mean = _seg_sum(y, bd) * (1.0 / HEAD)
    yc = y - mean
    var = _seg_sum(yc * yc, bd) * (1.0 / HEAD)
    yn = yc * lax.rsqrt(var + GN_EPS) * lnw_ref[...] + lnb_ref[...] + bonus_ref[...]
    gate = _dot(_sigmoid(zgl_ref[...].astype(F32)).astype(BF16), gup_ref[...])
    rwkv_out = _dot((yn * gate).astype(BF16), wrw_ref[...])

    ext_ref[0:POOL_HALO, :] = jnp.where(pos0 == 0, 0.0, zpprev_ref[...])
    ext_ref[POOL_HALO:POOL_HALO + tm, :] = zp_ref[...]
    ext_ref[POOL_HALO + tm:, :] = jnp.where(pos0 + tm == seq, 0.0, zpnext_ref[...])
    pos = pos0 + lax.broadcasted_iota(jnp.int32, (tm, 1), 0)
    pooled = []
    for g, w in enumerate(POOL_WINDOWS):
        cs = slice(g * POOL_GROUP, (g + 1) * POOL_GROUP)
        acc = ext_ref[POOL_HALO - w // 2:POOL_HALO - w // 2 + tm, cs]
        for j in range(1 - w // 2, w // 2):
            acc = acc + ext_ref[POOL_HALO + j:POOL_HALO + j + tm, cs]
        cnt = jnp.minimum(pos + (w // 2 - 1), seq - 1) - jnp.maximum(pos - w // 2, 0) + 1
        pg = acc / cnt.astype(F32) - zp_ref[:, cs]
        pooled.append(_dot(pg.astype(BF16), poolw_ref[g]))
    pooled = jnp.concatenate(pooled, axis=1) * pscale_ref[...]
    pool_out = _dot(pooled.astype(BF16), wpool_ref[...])

    gates = _sigmoid(zg_ref[...] + bgate_ref[...])
    merged = gates[:, :D] * pool_out + gates[:, D:] * rwkv_out
    o_ref[...] = x_ref[...] + _dot(merged.astype(BF16), wout_ref[...])


def _mix(y_f, y_b, bonus, zs, z_pool, z_gate, x, seq, tm, p):
    T = x.shape[0]
    row = lambda i: (i, 0)
    consts = [p["ln_w"], p["ln_b"], p["g_up"], p["w_rwkv_br"], p["pool_w"], p["pool_scale"], p["w_pool_br"],
              p["b_gate"], p["w_out"], p["bd"]]
    gate_lora_blk = (3 * D + LORA_COLS) // 128
    return pl.pallas_call(
        functools.partial(_mix_kernel, seq, tm),
        grid=(T // tm,),
        in_specs=[pl.BlockSpec((tm, D), row), pl.BlockSpec((tm, D), row), pl.BlockSpec((tm, D), row),
                  pl.BlockSpec((tm, 128), lambda i: (i, gate_lora_blk)),
                  pl.BlockSpec((tm, POOL_WIDTH), row)] + _halo_specs(tm, T, POOL_WIDTH)
                 + [pl.BlockSpec((tm, GATE_COLS), row), pl.BlockSpec((tm, D), row)]
                 + [_const_spec(c.shape) for c in consts],
        out_specs=pl.BlockSpec((tm, D), row),
        out_shape=jax.ShapeDtypeStruct((T, D), F32),
        scratch_shapes=[pltpu.VMEM((tm + 2 * POOL_HALO, POOL_WIDTH), F32)],
        compiler_params=_cparams("parallel"),
        name="mix",
    )(y_f, y_b, bonus, zs, z_pool, z_pool, z_pool, z_gate, x, *consts)


def _ffn_kernel(x_ref, gffn_ref, w1_ref, w2_ref, gfin_ref, o_ref):
    x = x_ref[...]
    hn = _rms(x, gffn_ref[...]).astype(BF16)
    h = jnp.maximum(_dot(hn, w1_ref[...]), 0.0)
    x2 = x + _dot((h * h).astype(BF16), w2_ref[...])
    o_ref[...] = _rms(x2, gfin_ref[...])


def _ffn(x, g_ffn, w1, w2, g_final, tm):
    T = x.shape[0]
    row = lambda i: (i, 0)
    return pl.pallas_call(
        _ffn_kernel,
        grid=(T // tm,),
        in_specs=[pl.BlockSpec((tm, D), row), _const_spec((1, D)), _const_spec(w1.shape),
                  _const_spec(w2.shape), _const_spec((1, D))],
        out_specs=pl.BlockSpec((tm, D), row),
        out_shape=jax.ShapeDtypeStruct((T, D), F32),
        compiler_params=_cparams("parallel"),
        name="ffn",
    )(x, g_ffn, w1, w2, g_final)


def _prepare_params(g_mix, w_in, b_gate, mu_prev, mu_next, pool_w, pool_scale, w_pool_br, k_k, k_a, r_k,
                    w0_f, w_up_f, a0_f, a_up_f, w0_b, w_up_b, a0_b, a_up_b, g_up, ln_w, ln_b, w_rwkv_br,
                    w_out, g_ffn, w_ff1, w_ff2, g_final):
    row = lambda a: a.reshape(1, -1).astype(F32)
    lora = w_up_f.shape[0]
    zeros = jnp.zeros((lora, 2 * D), F32)
    seg = np.arange(SEG) // HEAD
    t = np.arange(WKV_CHUNKS * CHUNK)
    same_chunk = (t[None, :] // CHUNK) == (t[:, None] // CHUNK)
    return {
        "g_mix": row(g_mix), "w_in": w_in.astype(BF16), "b_gate": row(b_gate),
        "mu_prev": row(mu_prev), "mu_next": row(mu_next),
        "pool_w": pool_w.astype(BF16), "pool_scale": row(pool_scale), "w_pool_br": w_pool_br.astype(BF16),
        "k_k": row(k_k), "k_a": row(k_a), "r_k": row(r_k),
        "w0": jnp.concatenate([row(w0_f), row(w0_b)], axis=1),
        "a0": jnp.concatenate([row(a0_f), row(a0_b)], axis=1),
        "w_up": jnp.concatenate([jnp.concatenate([w_up_f, w_up_b], axis=1), zeros], axis=0).astype(BF16),
        "a_up": jnp.concatenate([zeros, jnp.concatenate([a_up_f, a_up_b], axis=1)], axis=0).astype(BF16),
        "g_up": g_up.astype(BF16), "ln_w": row(ln_w), "ln_b": row(ln_b),
        "w_rwkv_br": w_rwkv_br.astype(BF16), "w_out": w_out.astype(BF16),
        "g_ffn": row(g_ffn), "w_ff1": w_ff1.astype(BF16), "w_ff2": w_ff2.astype(BF16), "g_final": row(g_final),
        "bd": jnp.asarray(seg[:, None] == seg[None, :], BF16),
        "tri_f": jnp.asarray(same_chunk & (t[None, :] <= t[:, None]), BF16),
        "tri_b": jnp.asarray(same_chunk & (t[None, :] >= t[:, None]), BF16),
    }


TM_PROJ = 512
WKV_CHUNKS = 4
TM_MIX = 512
TM_FFN = 512


def _trunk(x, p):
    batch, seq, _ = x.shape
    xf = x.reshape(batch * seq, D)
    z_pool, zs, z_gate = _in_proj(xf, seq, TM_PROJ, p)
    y_f, y_b, bonus = _wkv(zs, seq, WKV_CHUNKS, p)
    x1 = _mix(y_f, y_b, bonus, zs, z_pool, z_gate, xf, seq, TM_MIX, p)
    out = _ffn(x1, p["g_ffn"], p["w_ff1"], p["w_ff2"], p["g_final"], TM_FFN)
    return out.reshape(batch, seq, D)


def kernel(x_prompt, x_sample, g_mix, w_in, b_gate, mu_prev, mu_next, pool_w, pool_scale, w_pool_br, k_k, k_a, r_k, w0_f, w_up_f, a0_f, a_up_f, w0_b, w_up_b, a0_b, a_up_b, g_up, ln_w, ln_b, w_rwkv_br, w_out, g_ffn, w_ff1, w_ff2, g_final):
    depth = g_mix.shape[0]
    layers = [_prepare_params(g_mix[l], w_in[l], b_gate[l], mu_prev[l], mu_next[l], pool_w[l], pool_scale[l],
                              w_pool_br[l], k_k[l], k_a[l], r_k[l], w0_f[l], w_up_f[l], a0_f[l], a_up_f[l],
                              w0_b[l], w_up_b[l], a0_b[l], a_up_b[l], g_up[l], ln_w[l], ln_b[l], w_rwkv_br[l],
                              w_out[l], g_ffn[l], w_ff1[l], w_ff2[l], g_final) for l in range(depth)]
    assert depth == 1, "the final norm is fused into the last layer's ffn; only depth 1 is supported"
    return tuple(_trunk(x, layers[0]) for x in (x_prompt, x_sample))
```

```python
import functools
import math

import jax
import jax.numpy as jnp
import numpy as np
from jax import lax
from jax.experimental import pallas as pl
from jax.experimental.pallas import tpu as pltpu

D = 1024
HEAD = 64
N_HEADS = D // HEAD
POOL_WIDTH = 512
POOL_GROUP = 128
POOL_WINDOWS = (2, 4, 8, 16)
POOL_HALO = 8
LORA_COLS = 128
RW_COLS = 3 * D + 256
GATE_COLS = 2 * D
D_FF = 4 * D
RMS_EPS = 1e-6
GN_EPS = 64e-5
L2_EPS = 1e-12
CHUNK = 64
SEG = 256
VMEM_LIMIT = 56 * 1024 * 1024

F32 = jnp.float32
BF16 = jnp.bfloat16


def _dot(a, b):
    return jnp.dot(a, b, preferred_element_type=F32)


def _dot_nt(a, b):
    return lax.dot_general(a, b, (((1,), (1,)), ((), ())), preferred_element_type=F32)


def _dot_tn(a, b):
    return lax.dot_general(a, b, (((0,), (0,)), ((), ())), preferred_element_type=F32)


def _split(x):
    hi = x.astype(BF16)
    lo = (x - hi.astype(F32)).astype(BF16)
    return hi, lo


def _seg_sum(x, bd):
    rows, n_groups = x.shape[0], x.shape[1] // SEG
    xb = x.astype(BF16)
    stacked = jnp.concatenate([xb[:, g * SEG:(g + 1) * SEG] for g in range(n_groups)], axis=0)
    out = _dot(stacked, bd)
    return jnp.concatenate([out[g * rows:(g + 1) * rows] for g in range(n_groups)], axis=1)


def _rms(x, g):
    return x * lax.rsqrt(jnp.mean(x * x, axis=-1, keepdims=True) + RMS_EPS) * g


def _sigmoid(x):
    return 0.5 * jnp.tanh(0.5 * x) + 0.5


def _cparams(*sem):
    return pltpu.CompilerParams(dimension_semantics=sem, vmem_limit_bytes=VMEM_LIMIT)


def _const_spec(shape):
    nd = len(shape)
    return pl.BlockSpec(shape, lambda *_: (0,) * nd, pipeline_mode=pl.Buffered(1))


def _halo_specs(tm, n_rows, width):
    nb = tm // POOL_HALO
    last_blk = n_rows // POOL_HALO - 1
    return [pl.BlockSpec((POOL_HALO, width), lambda i: (jnp.maximum(i * nb - 1, 0), 0)),
            pl.BlockSpec((POOL_HALO, width), lambda i: (jnp.minimum((i + 1) * nb, last_blk), 0))]


def _in_proj_kernel(seq, tm, x_ref, xprev_ref, xnext_ref, g_ref, w_ref, mup_ref, mun_ref, bgate_ref,
                    poolw_ref, pscale_ref, pooled_ref, zr_ref, gates_ref, ext_ref):
    pos0 = (pl.program_id(0) * tm) % seq
    at_start = pos0 == 0
    at_end = pos0 + tm == seq
    rows = lax.broadcasted_iota(jnp.int32, (tm, 1), 0)
    x_all = jnp.concatenate([x_ref[...], xprev_ref[...], xnext_ref[...]], axis=0)
    xn_all = _rms(x_all, g_ref[...]).astype(BF16)
    xn = xn_all[:tm]

    gates_ref[...] = _sigmoid(_dot(xn, w_ref[:, POOL_WIDTH + RW_COLS:]) + bgate_ref[...]).astype(BF16)

    for c0 in range(0, RW_COLS, D):
        c1 = min(c0 + D, RW_COLS)
        z_all = _dot(xn_all, w_ref[:, POOL_WIDTH + c0:POOL_WIDTH + c1])
        z = z_all[:tm]
        prev_row = jnp.where(at_start, 0.0, z_all[tm + POOL_HALO - 1:tm + POOL_HALO])
        next_row = jnp.where(at_end, 0.0, z_all[tm + POOL_HALO:tm + POOL_HALO + 1])
        z_prev = jnp.where(rows == 0, prev_row, pltpu.roll(z, 1, axis=0))
        z_next = jnp.where(rows == tm - 1, next_row, pltpu.roll(z, tm - 1, axis=0))
        zs = z + mup_ref[:, c0:c1] * (z_prev - z) + mun_ref[:, c0:c1] * (z_next - z)
        zr_ref[:, c0:c1] = zs.astype(BF16)

    zp_all = _dot(xn_all, w_ref[:, 0:POOL_WIDTH])
    ext_ref[0:POOL_HALO, :] = jnp.where(at_start, 0.0, zp_all[tm:tm + POOL_HALO])
    ext_ref[POOL_HALO:POOL_HALO + tm, :] = zp_all[:tm]
    ext_ref[POOL_HALO + tm:, :] = jnp.where(at_end, 0.0, zp_all[tm + POOL_HALO:])
    pos = pos0 + rows
    pooled = []
    for g, w in enumerate(POOL_WINDOWS):
        cs = slice(g * POOL_GROUP, (g + 1) * POOL_GROUP)
        acc = ext_ref[POOL_HALO - w // 2:POOL_HALO - w // 2 + tm, cs]
        for j in range(1 - w // 2, w // 2):
            acc = acc + ext_ref[POOL_HALO + j:POOL_HALO + j + tm, cs]
        cnt = jnp.minimum(pos + (w // 2 - 1), seq - 1) - jnp.maximum(pos - w // 2, 0) + 1
        pg = acc * (1.0 / cnt.astype(F32)) - zp_all[:tm, cs]
        pooled.append(_dot(pg.astype(BF16), poolw_ref[g]))
    pooled_ref[...] = (jnp.concatenate(pooled, axis=1) * pscale_ref[...]).astype(BF16)


def _in_proj(x, seq, tm, p):
    T = x.shape[0]
    row = lambda i: (i, 0)
    consts = [p["g_mix"], p["w_in"], p["mu_prev"], p["mu_next"], p["b_gate"], p["pool_w"], p["pool_scale"]]
    return pl.pallas_call(
        functools.partial(_in_proj_kernel, seq, tm),
        grid=(T // tm,),
        in_specs=[pl.BlockSpec((tm, D), row)] + _halo_specs(tm, T, D) + [_const_spec(c.shape) for c in consts],
        out_specs=[pl.BlockSpec((tm, POOL_WIDTH), row), pl.BlockSpec((tm, RW_COLS), row),
                   pl.BlockSpec((tm, GATE_COLS), row)],
        out_shape=[jax.ShapeDtypeStruct((T, POOL_WIDTH), BF16), jax.ShapeDtypeStruct((T, RW_COLS), BF16),
                   jax.ShapeDtypeStruct((T, GATE_COLS), BF16)],
        scratch_shapes=[pltpu.VMEM((tm + 2 * POOL_HALO, POOL_WIDTH), F32)],
        compiler_params=_cparams("parallel"),
        name="in_proj",
    )(x, x, x, *consts)


def _bd(x, lo):
    zero = jnp.zeros_like(x)
    return jnp.concatenate([jnp.where(lo, x, zero), jnp.where(lo, zero, x)], axis=0)


def _undiag(full, lo):
    return jnp.where(lo, full[:CHUNK], full[CHUNK:])


def _wkv_kernel(n_chunks, zf_ref, zb_ref, kk_ref, ka_ref, rk_ref, w0_ref, a0_ref, wup_ref, aup_ref,
                bd_ref, trif_ref, trib_ref, yf_ref, yb_ref, bonus_ref, s_ref):
    @pl.when(pl.program_id(1) == 0)
    def _():
        s_ref[...] = jnp.zeros_like(s_ref)

    bd = bd_ref[...]
    k_k, k_a = kk_ref[...], ka_ref[...]
    y_refs = (yf_ref, yb_ref)
    n_pairs = D // (2 * HEAD)
    dir_cols = lambda d: slice(d * D, (d + 1) * D)
    lora_cols = slice(3 * D, 3 * D + LORA_COLS)

    def learning_rate(lora_in, d):
        return _sigmoid(a0_ref[:, dir_cols(d)] + _dot(lora_in, aup_ref[:, dir_cols(d)]))

    def chunk_operands(d, z_ref, tri_ref):
        r = z_ref[:, 0:D].astype(F32)
        k = z_ref[:, D:2 * D].astype(F32)
        lora_in = z_ref[:, lora_cols]
        kkr = k * k_k
        kk = kkr * jnp.minimum(lax.rsqrt(_seg_sum(kkr * kkr, bd)), 1.0 / L2_EPS)
        w_raw = w0_ref[:, dir_cols(d)] + _dot(jnp.tanh(lora_in.astype(F32)).astype(BF16),
                                              wup_ref[:, dir_cols(d)])
        lw = -_sigmoid(w_raw) * math.exp(-0.5)
        a = learning_rate(lora_in, d)
        kd = k * (1.0 + (a - 1.0) * k_a)
        tri = tri_ref[...]
        cum = _dot(jnp.concatenate([tri, tri], axis=1), jnp.concatenate(_split(lw), axis=0))
        e_neg = jnp.exp(-cum)
        last = CHUNK - 1 if d == 0 else 0
        ops = dict(rt=(r * jnp.exp(cum)).astype(BF16), bt=(kk * a * e_neg).astype(BF16),
                   kt=(kd * e_neg).astype(BF16), at=(-kk * jnp.exp(cum - lw)).astype(BF16),
                   wc=[jnp.exp(cum[c * CHUNK + last:c * CHUNK + last + 1, :]) for c in range(n_chunks)])
        return ops, r, k, kd

    ops_f, r_f, k_f, kd_f = chunk_operands(0, zf_ref, trif_ref)
    ops_b, _, _, _ = chunk_operands(1, zb_ref, trib_ref)
    ops_f["v"] = zf_ref[:, 2 * D:3 * D]
    ops_b["v"] = zb_ref[:, 2 * D:3 * D]
    ops = (ops_f, ops_b)
    kd_b_here = k_f * (1.0 + (learning_rate(zf_ref[:, lora_cols], 1) - 1.0) * k_a)
    bonus_ref[...] = _seg_sum(r_f * (kd_f + kd_b_here) * rk_ref[...], bd) * ops_f["v"].astype(F32)

    row = lax.broadcasted_iota(jnp.int32, (CHUNK, 2 * HEAD), 0)
    lane = lax.broadcasted_iota(jnp.int32, (CHUNK, 2 * HEAD), 1)
    lo = lane < HEAD
    col = jnp.where(lo, lane, lane - HEAD)
    eye = (row == col).astype(F32)
    insts = [(d, c, p) for d in range(2) for c in range(n_chunks) for p in range(n_pairs)]
    lanes = lambda p: slice(p * 128, (p + 1) * 128)
    rows_of = lambda c: slice(c * CHUNK, (c + 1) * CHUNK)
    get = lambda d, c, p, name: ops[d][name][rows_of(c), lanes(p)]

    xs, ts, a_ak, a_rb, a_rk = [], [], [], [], []
    for d, c, p in insts:
        strict = (row > col) if d == 0 else (row < col)
        incl = (row >= col) if d == 0 else (row <= col)
        lhs = jnp.concatenate([get(d, c, p, "at"), get(d, c, p, "rt")], axis=0)
        rhs = jnp.concatenate([_bd(get(d, c, p, "bt"), lo), _bd(get(d, c, p, "kt"), lo)], axis=0)
        out = _dot_nt(lhs, rhs)
        x = jnp.where(strict, out[:CHUNK, :128], 0.0)
        xs.append(x)
        ts.append(eye + x)
        a_ak.append(jnp.where(strict, out[:CHUNK, 128:], 0.0).astype(BF16))
        a_rb.append(jnp.where(incl, out[CHUNK:, :128], 0.0).astype(BF16))
        a_rk.append(jnp.where(incl, out[CHUNK:, 128:], 0.0).astype(BF16))

    n_steps = int(math.log2(CHUNK))
    for j in range(n_steps):
        for i in range(len(insts)):
            xb = xs[i].astype(BF16)
            rhs = _bd(xb, lo)
            if j == 0:
                xs[i] = _dot(xb, rhs)
            elif j < n_steps - 1:
                out = _dot(jnp.concatenate([xb, ts[i].astype(BF16)], axis=0), rhs)
                xs[i] = out[:CHUNK]
                ts[i] = ts[i] + out[CHUNK:]
            else:
                ts[i] = ts[i] + _dot(ts[i].astype(BF16), rhs)

    akv, arkv = [], []
    for i, (d, c, p) in enumerate(insts):
        out = _dot(jnp.concatenate([a_ak[i], a_rk[i]], axis=0), _bd(get(d, c, p, "v"), lo))
        akv.append(out[:CHUNK].astype(BF16))
        arkv.append(out[CHUNK:])

    ps, qs = [], []
    for i, (d, c, p) in enumerate(insts):
        rhs = jnp.concatenate([_bd(get(d, c, p, "at"), lo), _bd(akv[i], lo)], axis=1)
        pq = _dot(ts[i].astype(BF16), rhs)
        ps.append(pq[:, :128].astype(BF16))
        qs.append(pq[:, 128:].astype(BF16))

    rps, y0s = [], []
    for i, (d, c, p) in enumerate(insts):
        ry = _dot(a_rb[i], jnp.concatenate([_bd(ps[i], lo), _bd(qs[i], lo)], axis=1))
        rps.append((get(d, c, p, "rt").astype(F32) + ry[:, :128]).astype(BF16))
        y0s.append(ry[:, 128:] + arkv[i])

    ms, ns = [], []
    for i, (d, c, p) in enumerate(insts):
        wc = ops[d]["wc"][c][:, lanes(p)]
        b_end = (get(d, c, p, "bt").astype(F32) * wc).astype(BF16)
        k_end = (get(d, c, p, "kt").astype(F32) * wc).astype(BF16)
        lhs_t = jnp.concatenate([b_end, k_end], axis=0)
        v = get(d, c, p, "v")
        rhs = jnp.concatenate([jnp.concatenate([ps[i], qs[i]], axis=1),
                               jnp.concatenate([jnp.zeros_like(v), v], axis=1)], axis=0)
        full = _dot_tn(lhs_t, rhs)
        ms.append((eye * wc + _undiag(full[:, :128], lo)).astype(BF16))
        ns.append(_undiag(full[:, 128:], lo))

    for step in range(n_chunks):
        for i, (d, c, p) in enumerate(insts):
            if c != (step if d == 0 else n_chunks - 1 - step):
                continue
            s0 = s_ref[d, :, lanes(p)]
            out = _dot(jnp.concatenate([ms[i], rps[i]], axis=0), _bd(s0.astype(BF16), lo))
            s_ref[d, :, lanes(p)] = out[:CHUNK] + ns[i]
            y_refs[d][rows_of(c), lanes(p)] = out[CHUNK:] + y0s[i]


def _wkv(zs, seq, n_chunks, p):
    T = zs.shape[0]
    tm = n_chunks * CHUNK
    nc = seq // tm
    pos = (lambda b, c: (b * nc + c, 0), lambda b, c: (b * nc + nc - 1 - c, 0))
    params = [p["k_k"], p["k_a"], p["r_k"], p["w0"], p["a0"], p["w_up"], p["a_up"], p["bd"], p["tri_f"],
              p["tri_b"]]
    return pl.pallas_call(
        functools.partial(_wkv_kernel, n_chunks),
        grid=(T // seq, nc),
        in_specs=[pl.BlockSpec((tm, RW_COLS), pos[0]), pl.BlockSpec((tm, RW_COLS), pos[1])]
                 + [_const_spec(c.shape) for c in params],
        out_specs=[pl.BlockSpec((tm, D), pos[0]), pl.BlockSpec((tm, D), pos[1]),
                   pl.BlockSpec((tm, D), pos[0])],
        out_shape=[jax.ShapeDtypeStruct((T, D), F32)] * 3,
        scratch_shapes=[pltpu.VMEM((2, CHUNK, D), F32)],
        compiler_params=_cparams("parallel", "arbitrary"),
        name="wkv",
    )(zs, zs, *params)


def _out_kernel(yf_ref, yb_ref, bonus_ref, zgl_ref, pooled_ref, gates_ref, x_ref,
                lnw_ref, lnb_ref, gup_ref, wrw_ref, wpool_ref, wout_ref, bd_ref,
                gffn_ref, w1_ref, w2_ref, gfin_ref, o_ref):
    bd_mean = bd_ref[...]

    y = yf_ref[...] + yb_ref[...]
    yc = y - _seg_sum(y, bd_mean)
    var = _seg_sum(yc * yc, bd_mean)
    yn = yc * lax.rsqrt(var + GN_EPS) * lnw_ref[...] + lnb_ref[...] + bonus_ref[...]
    gate = _dot(_sigmoid(zgl_ref[...].astype(F32)).astype(BF16), gup_ref[...])
    rwkv_out = _dot((yn * gate).astype(BF16), wrw_ref[...])

    pool_out = _dot(pooled_ref[...], wpool_ref[...])
    gates = gates_ref[...].astype(F32)
    merged = gates[:, :D] * pool_out + gates[:, D:] * rwkv_out
    x = x_ref[...] + _dot(merged.astype(BF16), wout_ref[...])

    hn = _rms(x, gffn_ref[...]).astype(BF16)
    h = jnp.maximum(_dot(hn, w1_ref[...]), 0.0)
    x2 = x + _dot((h * h).astype(BF16), w2_ref[...])
    o_ref[...] = _rms(x2, gfin_ref[...])


def _out(y_f, y_b, bonus, zs, pooled, gates, x, tm, p):
    T = x.shape[0]
    row = lambda i: (i, 0)
    consts = [p["ln_w"], p["ln_b"], p["g_up"], p["w_rwkv_br"], p["w_pool_br"], p["w_out"], p["bd_mean"],
              p["g_ffn"], p["w_ff1"], p["w_ff2"], p["g_final"]]
    gate_lora_blk = (3 * D + LORA_COLS) // 128
    return pl.pallas_call(
        _out_kernel,
        grid=(T // tm,),
        in_specs=[pl.BlockSpec((tm, D), row), pl.BlockSpec((tm, D), row), pl.BlockSpec((tm, D), row),
                  pl.BlockSpec((tm, 128), lambda i: (i, gate_lora_blk)),
                  pl.BlockSpec((tm, POOL_WIDTH), row), pl.BlockSpec((tm, GATE_COLS), row),
                  pl.BlockSpec((tm, D), row)]
                 + [_const_spec(c.shape) for c in consts],
        out_specs=pl.BlockSpec((tm, D), row),
        out_shape=jax.ShapeDtypeStruct((T, D), F32),
        compiler_params=_cparams("parallel"),
        name="out",
    )(y_f, y_b, bonus, zs, pooled, gates, x, *consts)


def _prepare_params(g_mix, w_in, b_gate, mu_prev, mu_next, pool_w, pool_scale, w_pool_br, k_k, k_a, r_k,
                    w0_f, w_up_f, a0_f, a_up_f, w0_b, w_up_b, a0_b, a_up_b, g_up, ln_w, ln_b, w_rwkv_br,
                    w_out, g_ffn, w_ff1, w_ff2, g_final):
    row = lambda a: a.reshape(1, -1).astype(F32)
    lora = w_up_f.shape[0]
    zeros = jnp.zeros((lora, 2 * D), F32)
    seg = np.arange(SEG) // HEAD
    t = np.arange(WKV_CHUNKS * CHUNK)
    same_chunk = (t[None, :] // CHUNK) == (t[:, None] // CHUNK)
    return {
        "g_mix": row(g_mix), "w_in": w_in.astype(BF16), "b_gate": row(b_gate),
        "mu_prev": row(mu_prev), "mu_next": row(mu_next),
        "pool_w": pool_w.astype(BF16), "pool_scale": row(pool_scale), "w_pool_br": w_pool_br.astype(BF16),
        "k_k": row(k_k), "k_a": row(k_a), "r_k": row(r_k),
        "w0": jnp.concatenate([row(w0_f), row(w0_b)], axis=1),
        "a0": jnp.concatenate([row(a0_f), row(a0_b)], axis=1),
        "w_up": jnp.concatenate([jnp.concatenate([w_up_f, w_up_b], axis=1), zeros], axis=0).astype(BF16),
        "a_up": jnp.concatenate([zeros, jnp.concatenate([a_up_f, a_up_b], axis=1)], axis=0).astype(BF16),
        "g_up": g_up.astype(BF16), "ln_w": row(ln_w), "ln_b": row(ln_b),
        "w_rwkv_br": w_rwkv_br.astype(BF16), "w_out": w_out.astype(BF16),
        "g_ffn": row(g_ffn), "w_ff1": w_ff1.astype(BF16), "w_ff2": w_ff2.astype(BF16), "g_final": row(g_final),
        "bd": jnp.asarray(seg[:, None] == seg[None, :], BF16),
        "bd_mean": jnp.asarray((seg[:, None] == seg[None, :]) / HEAD, BF16),
        "tri_f": jnp.asarray(same_chunk & (t[None, :] <= t[:, None]), BF16),
        "tri_b": jnp.asarray(same_chunk & (t[None, :] >= t[:, None]), BF16),
    }


TM_PROJ = 512
WKV_CHUNKS = 4
TM_OUT = 512


def _trunk(x, p):
    batch, seq, _ = x.shape
    xf = x.reshape(batch * seq, D)
    pooled, zs, gates = _in_proj(xf, seq, TM_PROJ, p)
    y_f, y_b, bonus = _wkv(zs, seq, WKV_CHUNKS, p)
    out = _out(y_f, y_b, bonus, zs, pooled, gates, xf, TM_OUT, p)
    return out.reshape(batch, seq, D)


def kernel(x_prompt, x_sample, g_mix, w_in, b_gate, mu_prev, mu_next, pool_w, pool_scale, w_pool_br, k_k, k_a, r_k, w0_f, w_up_f, a0_f, a_up_f, w0_b, w_up_b, a0_b, a_up_b, g_up, ln_w, ln_b, w_rwkv_br, w_out, g_ffn, w_ff1, w_ff2, g_final):
    depth = g_mix.shape[0]
    layers = [_prepare_params(g_mix[l], w_in[l], b_gate[l], mu_prev[l], mu_next[l], pool_w[l], pool_scale[l],
                              w_pool_br[l], k_k[l], k_a[l], r_k[l], w0_f[l], w_up_f[l], a0_f[l], a_up_f[l],
                              w0_b[l], w_up_b[l], a0_b[l], a_up_b[l], g_up[l], ln_w[l], ln_b[l], w_rwkv_br[l],
                              w_out[l], g_ffn[l], w_ff1[l], w_ff2[l], g_final) for l in range(depth)]
    assert depth == 1, "the final norm is fused into the last layer's ffn; only depth 1 is supported"
    return tuple(_trunk(x, layers[0]) for x in (x_prompt, x_sample))
```

```python
import functools
import math

import jax
import jax.numpy as jnp
import numpy as np
from jax import lax
from jax.experimental import pallas as pl
from jax.experimental.pallas import tpu as pltpu

D = 1024
HEAD = 64
POOL_WIDTH = 512
POOL_GROUP = 128
POOL_WINDOWS = (2, 4, 8, 16)
POOL_HALO = 8
LORA_COLS = 128
GATE_LORA_COLS = 128
RW_COLS = 3 * D + LORA_COLS + GATE_LORA_COLS
GATE_COLS = 2 * D
RMS_EPS = 1e-6
GN_EPS = 64e-5
L2_EPS = 1e-12
CHUNK = 64
PAIR = 2 * HEAD
SEG = 256
VMEM_LIMIT = 56 * 1024 * 1024

F32 = jnp.float32
BF16 = jnp.bfloat16


def _dot(a, b):
    return jnp.dot(a, b, preferred_element_type=F32)


def _dot_nt(a, b):
    return lax.dot_general(a, b, (((1,), (1,)), ((), ())), preferred_element_type=F32)


def _dot_tn(a, b):
    return lax.dot_general(a, b, (((0,), (0,)), ((), ())), preferred_element_type=F32)


def _split(x):
    hi = x.astype(BF16)
    lo = (x - hi.astype(F32)).astype(BF16)
    return hi, lo


def _seg_sum(x, bd):
    rows, n_groups = x.shape[0], x.shape[1] // SEG
    xb = x.astype(BF16)
    stacked = jnp.concatenate([xb[:, g * SEG:(g + 1) * SEG] for g in range(n_groups)], axis=0)
    out = _dot(stacked, bd)
    return jnp.concatenate([out[g * rows:(g + 1) * rows] for g in range(n_groups)], axis=1)


def _rms(x, g):
    return x * lax.rsqrt(jnp.mean(x * x, axis=-1, keepdims=True) + RMS_EPS) * g


def _sigmoid(x):
    return 0.5 * jnp.tanh(0.5 * x) + 0.5


def _cparams(*sem):
    return pltpu.CompilerParams(dimension_semantics=sem, vmem_limit_bytes=VMEM_LIMIT)


def _const_spec(shape):
    nd = len(shape)
    return pl.BlockSpec(shape, lambda *_: (0,) * nd, pipeline_mode=pl.Buffered(1))


def _halo_specs(tm, n_rows, width):
    nb = tm // POOL_HALO
    last_blk = n_rows // POOL_HALO - 1
    return [pl.BlockSpec((POOL_HALO, width), lambda i: (jnp.maximum(i * nb - 1, 0), 0)),
            pl.BlockSpec((POOL_HALO, width), lambda i: (jnp.minimum((i + 1) * nb, last_blk), 0))]


def _in_proj_kernel(seq, tm, x_ref, xprev_ref, xnext_ref, g_ref, w_ref, mup_ref, mun_ref, bgate_ref,
                    poolw_ref, pscale_ref, pooled_ref, zr_ref, gates_ref, ext_ref):
    pos0 = (pl.program_id(0) * tm) % seq
    at_start = pos0 == 0
    at_end = pos0 + tm == seq
    rows = lax.broadcasted_iota(jnp.int32, (tm, 1), 0)
    x_all = jnp.concatenate([x_ref[...], xprev_ref[...], xnext_ref[...]], axis=0)
    xn_all = _rms(x_all, g_ref[...]).astype(BF16)
    xn = xn_all[:tm]

    zp_all = _dot(xn_all, w_ref[:, 0:POOL_WIDTH])
    ext_ref[0:POOL_HALO, :] = jnp.where(at_start, 0.0, zp_all[tm:tm + POOL_HALO])
    ext_ref[POOL_HALO:POOL_HALO + tm, :] = zp_all[:tm]
    ext_ref[POOL_HALO + tm:, :] = jnp.where(at_end, 0.0, zp_all[tm + POOL_HALO:])

    gates_ref[...] = _sigmoid(_dot(xn, w_ref[:, POOL_WIDTH + RW_COLS:]) + bgate_ref[...]).astype(BF16)

    for c0 in range(0, RW_COLS, D):
        c1 = min(c0 + D, RW_COLS)
        z_all = _dot(xn_all, w_ref[:, POOL_WIDTH + c0:POOL_WIDTH + c1])
        z = z_all[:tm]
        prev_row = jnp.where(at_start, 0.0, z_all[tm + POOL_HALO - 1:tm + POOL_HALO])
        next_row = jnp.where(at_end, 0.0, z_all[tm + POOL_HALO:tm + POOL_HALO + 1])
        z_prev = jnp.where(rows == 0, prev_row, pltpu.roll(z, 1, axis=0))
        z_next = jnp.where(rows == tm - 1, next_row, pltpu.roll(z, tm - 1, axis=0))
        zs = z + mup_ref[:, c0:c1] * (z_prev - z) + mun_ref[:, c0:c1] * (z_next - z)
        zr_ref[:, c0:c1] = zs.astype(BF16)

    pos = pos0 + rows
    pooled = []
    for g, w in enumerate(POOL_WINDOWS):
        cs = slice(g * POOL_GROUP, (g + 1) * POOL_GROUP)
        acc = ext_ref[POOL_HALO - w // 2:POOL_HALO - w // 2 + tm, cs]
        for j in range(1 - w // 2, w // 2):
            acc = acc + ext_ref[POOL_HALO + j:POOL_HALO + j + tm, cs]
        cnt = jnp.minimum(pos + (w // 2 - 1), seq - 1) - jnp.maximum(pos - w // 2, 0) + 1
        pg = acc * (1.0 / cnt.astype(F32)) - ext_ref[POOL_HALO:POOL_HALO + tm, cs]
        pooled.append(_dot(pg.astype(BF16), poolw_ref[g]))
    pooled_ref[...] = (jnp.concatenate(pooled, axis=1) * pscale_ref[...]).astype(BF16)


def _in_proj(x, seq, tm, p):
    T = x.shape[0]
    row = lambda i: (i, 0)
    consts = [p["g_mix"], p["w_in"], p["mu_prev"], p["mu_next"], p["b_gate"], p["pool_w"], p["pool_scale"]]
    return pl.pallas_call(
        functools.partial(_in_proj_kernel, seq, tm),
        grid=(T // tm,),
        in_specs=[pl.BlockSpec((tm, D), row)] + _halo_specs(tm, T, D) + [_const_spec(c.shape) for c in consts],
        out_specs=[pl.BlockSpec((tm, POOL_WIDTH), row), pl.BlockSpec((tm, RW_COLS), row),
                   pl.BlockSpec((tm, GATE_COLS), row)],
        out_shape=[jax.ShapeDtypeStruct((T, POOL_WIDTH), BF16), jax.ShapeDtypeStruct((T, RW_COLS), BF16),
                   jax.ShapeDtypeStruct((T, GATE_COLS), BF16)],
        scratch_shapes=[pltpu.VMEM((tm + 2 * POOL_HALO, POOL_WIDTH), F32)],
        compiler_params=_cparams("parallel"),
        name="in_proj",
    )(x, x, x, *consts)


def _bd(x, lo):
    zero = jnp.zeros_like(x)
    return jnp.concatenate([jnp.where(lo, x, zero), jnp.where(lo, zero, x)], axis=0)


def _undiag(full, lo):
    return jnp.where(lo, full[:CHUNK], full[CHUNK:])


def _wkv_kernel(n_chunks, zf_ref, zb_ref, kk_ref, ka_ref, rk_ref, w0_ref, a0_ref, wup_ref, aup_ref,
                bd_ref, trif_ref, trib_ref, yf_ref, yb_ref, bonus_ref, s_ref):
    @pl.when(pl.program_id(1) == 0)
    def _():
        s_ref[...] = jnp.zeros_like(s_ref)

    bd = bd_ref[...]
    k_k, k_a = kk_ref[...], ka_ref[...]
    y_refs = (yf_ref, yb_ref)
    n_pairs = D // PAIR
    dir_cols = lambda d: slice(d * D, (d + 1) * D)
    lora_cols = slice(3 * D, 3 * D + LORA_COLS)

    def learning_rate(lora_in, d):
        return _sigmoid(a0_ref[:, dir_cols(d)] + _dot(lora_in, aup_ref[:, dir_cols(d)]))

    def chunk_operands(d, z_ref, tri_ref):
        r = z_ref[:, 0:D].astype(F32)
        k = z_ref[:, D:2 * D].astype(F32)
        lora_in = z_ref[:, lora_cols]
        kkr = k * k_k
        kk = kkr * jnp.minimum(lax.rsqrt(_seg_sum(kkr * kkr, bd)), 1.0 / L2_EPS)
        w_raw = w0_ref[:, dir_cols(d)] + _dot(jnp.tanh(lora_in.astype(F32)).astype(BF16),
                                              wup_ref[:, dir_cols(d)])
        lw = -_sigmoid(w_raw) * math.exp(-0.5)
        a = learning_rate(lora_in, d)
        kd = k * (1.0 + (a - 1.0) * k_a)
        tri = tri_ref[...]
        cum = _dot(jnp.concatenate([tri, tri], axis=1), jnp.concatenate(_split(lw), axis=0))
        e_neg = jnp.exp(-cum)
        last = CHUNK - 1 if d == 0 else 0
        ops = dict(rt=(r * jnp.exp(cum)).astype(BF16), bt=(kk * a * e_neg).astype(BF16),
                   kt=(kd * e_neg).astype(BF16), at=(-kk * jnp.exp(cum - lw)).astype(BF16),
                   wc=[jnp.exp(cum[c * CHUNK + last:c * CHUNK + last + 1, :]) for c in range(n_chunks)])
        return ops, r, k, kd

    ops_f, r_f, k_f, kd_f = chunk_operands(0, zf_ref, trif_ref)
    ops_b, _, _, _ = chunk_operands(1, zb_ref, trib_ref)
    ops_f["v"] = zf_ref[:, 2 * D:3 * D]
    ops_b["v"] = zb_ref[:, 2 * D:3 * D]
    ops = (ops_f, ops_b)
    kd_b_here = k_f * (1.0 + (learning_rate(zf_ref[:, lora_cols], 1) - 1.0) * k_a)
    bonus_ref[...] = _seg_sum(r_f * (kd_f + kd_b_here) * rk_ref[...], bd) * ops_f["v"].astype(F32)

    row = lax.broadcasted_iota(jnp.int32, (CHUNK, PAIR), 0)
    lane = lax.broadcasted_iota(jnp.int32, (CHUNK, PAIR), 1)
    lo = lane < HEAD
    col = jnp.where(lo, lane, lane - HEAD)
    eye = (row == col).astype(F32)
    insts = [(d, c, p) for d in range(2) for c in range(n_chunks) for p in range(n_pairs)]
    lanes = lambda p: slice(p * PAIR, (p + 1) * PAIR)
    rows_of = lambda c: slice(c * CHUNK, (c + 1) * CHUNK)
    get = lambda d, c, p, name: ops[d][name][rows_of(c), lanes(p)]

    xs, ts, a_ak, a_rb, a_rk = [], [], [], [], []
    for d, c, p in insts:
        strict = (row > col) if d == 0 else (row < col)
        incl = (row >= col) if d == 0 else (row <= col)
        lhs = jnp.concatenate([get(d, c, p, "at"), get(d, c, p, "rt")], axis=0)
        rhs = jnp.concatenate([_bd(get(d, c, p, "bt"), lo), _bd(get(d, c, p, "kt"), lo)], axis=0)
        out = _dot_nt(lhs, rhs)
        x = jnp.where(strict, out[:CHUNK, :PAIR], 0.0)
        xs.append(x)
        ts.append(eye + x)
        a_ak.append(jnp.where(strict, out[:CHUNK, PAIR:], 0.0).astype(BF16))
        a_rb.append(jnp.where(incl, out[CHUNK:, :PAIR], 0.0).astype(BF16))
        a_rk.append(jnp.where(incl, out[CHUNK:, PAIR:], 0.0).astype(BF16))

    n_steps = int(math.log2(CHUNK))
    for j in range(n_steps):
        for i in range(len(insts)):
            xb = xs[i].astype(BF16)
            rhs = _bd(xb, lo)
            if j == 0:
                xs[i] = _dot(xb, rhs)
            elif j < n_steps - 1:
                out = _dot(jnp.concatenate([xb, ts[i].astype(BF16)], axis=0), rhs)
                xs[i] = out[:CHUNK]
                ts[i] = ts[i] + out[CHUNK:]
            else:
                ts[i] = ts[i] + _dot(ts[i].astype(BF16), rhs)

    akv, arkv = [], []
    for i, (d, c, p) in enumerate(insts):
        out = _dot(jnp.concatenate([a_ak[i], a_rk[i]], axis=0), _bd(get(d, c, p, "v"), lo))
        akv.append(out[:CHUNK].astype(BF16))
        arkv.append(out[CHUNK:])

    ps, qs = [], []
    for i, (d, c, p) in enumerate(insts):
        rhs = jnp.concatenate([_bd(get(d, c, p, "at"), lo), _bd(akv[i], lo)], axis=1)
        pq = _dot(ts[i].astype(BF16), rhs)
        ps.append(pq[:, :PAIR].astype(BF16))
        qs.append(pq[:, PAIR:].astype(BF16))

    rps, y0s = [], []
    for i, (d, c, p) in enumerate(insts):
        ry = _dot(a_rb[i], jnp.concatenate([_bd(ps[i], lo), _bd(qs[i], lo)], axis=1))
        rps.append((get(d, c, p, "rt").astype(F32) + ry[:, :PAIR]).astype(BF16))
        y0s.append(ry[:, PAIR:] + arkv[i])

    ms, ns = [], []
    for i, (d, c, p) in enumerate(insts):
        wc = ops[d]["wc"][c][:, lanes(p)]
        b_end = (get(d, c, p, "bt").astype(F32) * wc).astype(BF16)
        k_end = (get(d, c, p, "kt").astype(F32) * wc).astype(BF16)
        lhs_t = jnp.concatenate([b_end, k_end], axis=0)
        v = get(d, c, p, "v")
        rhs = jnp.concatenate([jnp.concatenate([ps[i], qs[i]], axis=1),
                               jnp.concatenate([jnp.zeros_like(v), v], axis=1)], axis=0)
        full = _dot_tn(lhs_t, rhs)
        ms.append((eye * wc + _undiag(full[:, :PAIR], lo)).astype(BF16))
        ns.append(_undiag(full[:, PAIR:], lo))

    for step in range(n_chunks):
        for i, (d, c, p) in enumerate(insts):
            if c != (step if d == 0 else n_chunks - 1 - step):
                continue
            s0 = s_ref[d, :, lanes(p)]
            out = _dot(jnp.concatenate([ms[i], rps[i]], axis=0), _bd(s0.astype(BF16), lo))
            s_ref[d, :, lanes(p)] = out[:CHUNK] + ns[i]
            y_refs[d][rows_of(c), lanes(p)] = out[CHUNK:] + y0s[i]


def _wkv(zs, seq, n_chunks, p):
    T = zs.shape[0]
    tm = n_chunks * CHUNK
    nc = seq // tm
    pos = (lambda b, c: (b * nc + c, 0), lambda b, c: (b * nc + nc - 1 - c, 0))
    params = [p["k_k"], p["k_a"], p["r_k"], p["w0"], p["a0"], p["w_up"], p["a_up"], p["bd"], p["tri_f"],
              p["tri_b"]]
    return pl.pallas_call(
        functools.partial(_wkv_kernel, n_chunks),
        grid=(T // seq, nc),
        in_specs=[pl.BlockSpec((tm, RW_COLS), pos[0]), pl.BlockSpec((tm, RW_COLS), pos[1])]
                 + [_const_spec(c.shape) for c in params],
        out_specs=[pl.BlockSpec((tm, D), pos[0]), pl.BlockSpec((tm, D), pos[1]),
                   pl.BlockSpec((tm, D), pos[0])],
        out_shape=[jax.ShapeDtypeStruct((T, D), F32)] * 3,
        scratch_shapes=[pltpu.VMEM((2, CHUNK, D), F32)],
        compiler_params=_cparams("parallel", "arbitrary"),
        name="wkv",
    )(zs, zs, *params)


def _out_kernel(yf_ref, yb_ref, bonus_ref, zgl_ref, pooled_ref, gates_ref, x_ref,
                lnw_ref, lnb_ref, gup_ref, wrw_ref, wpool_ref, wout_ref, bd_ref,
                gffn_ref, w1_ref, w2_ref, gfin_ref, o_ref):
    bd_mean = bd_ref[...]

    y = yf_ref[...] + yb_ref[...]
    yc = y - _seg_sum(y, bd_mean)
    var = _seg_sum(yc * yc, bd_mean)
    yn = yc * lax.rsqrt(var + GN_EPS) * lnw_ref[...] + lnb_ref[...] + bonus_ref[...]
    gate = _dot(_sigmoid(zgl_ref[...].astype(F32)).astype(BF16), gup_ref[...])
    rwkv_out = _dot((yn * gate).astype(BF16), wrw_ref[...])

    pool_out = _dot(pooled_ref[...], wpool_ref[...])
    gates = gates_ref[...].astype(F32)
    merged = gates[:, :D] * pool_out + gates[:, D:] * rwkv_out
    x = x_ref[...] + _dot(merged.astype(BF16), wout_ref[...])

    hn = _rms(x, gffn_ref[...]).astype(BF16)
    h = jnp.maximum(_dot(hn, w1_ref[...]), 0.0)
    x2 = x + _dot((h * h).astype(BF16), w2_ref[...])
    o_ref[...] = _rms(x2, gfin_ref[...])


def _out(y_f, y_b, bonus, zs, pooled, gates, x, tm, p):
    T = x.shape[0]
    row = lambda i: (i, 0)
    consts = [p["ln_w"], p["ln_b"], p["g_up"], p["w_rwkv_br"], p["w_pool_br"], p["w_out"], p["bd_mean"],
              p["g_ffn"], p["w_ff1"], p["w_ff2"], p["g_final"]]
    gate_lora_blk = (3 * D + LORA_COLS) // GATE_LORA_COLS
    return pl.pallas_call(
        _out_kernel,
        grid=(T // tm,),
        in_specs=[pl.BlockSpec((tm, D), row), pl.BlockSpec((tm, D), row), pl.BlockSpec((tm, D), row),
                  pl.BlockSpec((tm, GATE_LORA_COLS), lambda i: (i, gate_lora_blk)),
                  pl.BlockSpec((tm, POOL_WIDTH), row), pl.BlockSpec((tm, GATE_COLS), row),
                  pl.BlockSpec((tm, D), row)]
                 + [_const_spec(c.shape) for c in consts],
        out_specs=pl.BlockSpec((tm, D), row),
        out_shape=jax.ShapeDtypeStruct((T, D), F32),
        compiler_params=_cparams("parallel"),
        name="out",
    )(y_f, y_b, bonus, zs, pooled, gates, x, *consts)


def _prepare_params(g_mix, w_in, b_gate, mu_prev, mu_next, pool_w, pool_scale, w_pool_br, k_k, k_a, r_k,
                    w0_f, w_up_f, a0_f, a_up_f, w0_b, w_up_b, a0_b, a_up_b, g_up, ln_w, ln_b, w_rwkv_br,
                    w_out, g_ffn, w_ff1, w_ff2, g_final):
    row = lambda a: a.reshape(1, -1).astype(F32)
    lora = w_up_f.shape[0]
    zeros = jnp.zeros((lora, 2 * D), F32)
    seg = np.arange(SEG) // HEAD
    t = np.arange(WKV_CHUNKS * CHUNK)
    same_chunk = (t[None, :] // CHUNK) == (t[:, None] // CHUNK)
    return {
        "g_mix": row(g_mix), "w_in": w_in.astype(BF16), "b_gate": row(b_gate),
        "mu_prev": row(mu_prev), "mu_next": row(mu_next),
        "pool_w": pool_w.astype(BF16), "pool_scale": row(pool_scale), "w_pool_br": w_pool_br.astype(BF16),
        "k_k": row(k_k), "k_a": row(k_a), "r_k": row(r_k),
        "w0": jnp.concatenate([row(w0_f), row(w0_b)], axis=1),
        "a0": jnp.concatenate([row(a0_f), row(a0_b)], axis=1),
        "w_up": jnp.concatenate([jnp.concatenate([w_up_f, w_up_b], axis=1), zeros], axis=0).astype(BF16),
        "a_up": jnp.concatenate([zeros, jnp.concatenate([a_up_f, a_up_b], axis=1)], axis=0).astype(BF16),
        "g_up": g_up.astype(BF16), "ln_w": row(ln_w), "ln_b": row(ln_b),
        "w_rwkv_br": w_rwkv_br.astype(BF16), "w_out": w_out.astype(BF16),
        "g_ffn": row(g_ffn), "w_ff1": w_ff1.astype(BF16), "w_ff2": w_ff2.astype(BF16), "g_final": row(g_final),
        "bd": jnp.asarray(seg[:, None] == seg[None, :], BF16),
        "bd_mean": jnp.asarray((seg[:, None] == seg[None, :]) / HEAD, BF16),
        "tri_f": jnp.asarray(same_chunk & (t[None, :] <= t[:, None]), BF16),
        "tri_b": jnp.asarray(same_chunk & (t[None, :] >= t[:, None]), BF16),
    }


TM_PROJ = 1024
WKV_CHUNKS = 4
TM_OUT = 512


def _trunk(x, p):
    batch, seq, _ = x.shape
    xf = x.reshape(batch * seq, D)
    pooled, zs, gates = _in_proj(xf, seq, TM_PROJ, p)
    y_f, y_b, bonus = _wkv(zs, seq, WKV_CHUNKS, p)
    out = _out(y_f, y_b, bonus, zs, pooled, gates, xf, TM_OUT, p)
    return out.reshape(batch, seq, D)


def kernel(x_prompt, x_sample, g_mix, w_in, b_gate, mu_prev, mu_next, pool_w, pool_scale, w_pool_br, k_k, k_a, r_k, w0_f, w_up_f, a0_f, a_up_f, w0_b, w_up_b, a0_b, a_up_b, g_up, ln_w, ln_b, w_rwkv_br, w_out, g_ffn, w_ff1, w_ff2, g_final):
    depth = g_mix.shape[0]
    layers = [_prepare_params(g_mix[l], w_in[l], b_gate[l], mu_prev[l], mu_next[l], pool_w[l], pool_scale[l],
                              w_pool_br[l], k_k[l], k_a[l], r_k[l], w0_f[l], w_up_f[l], a0_f[l], a_up_f[l],
                              w0_b[l], w_up_b[l], a0_b[l], a_up_b[l], g_up[l], ln_w[l], ln_b[l], w_rwkv_br[l],
                              w_out[l], g_ffn[l], w_ff1[l], w_ff2[l], g_final) for l in range(depth)]
    assert depth == 1, "the final norm is fused into the last layer's ffn; only depth 1 is supported"
    return tuple(_trunk(x, layers[0]) for x in (x_prompt, x_sample))
```

```python
import functools
import math

import jax
import jax.numpy as jnp
import numpy as np
from jax import lax
from jax.experimental import pallas as pl
from jax.experimental.pallas import tpu as pltpu

D = 1024
HEAD = 64
POOL_WIDTH = 512
POOL_GROUP = 128
POOL_WINDOWS = (2, 4, 8, 16)
POOL_HALO = 8
LORA_COLS = 128
GATE_LORA_COLS = 128
RW_COLS = 3 * D + LORA_COLS + GATE_LORA_COLS
GATE_COLS = 2 * D
RMS_EPS = 1e-6
GN_EPS = 64e-5
L2_EPS = 1e-12
CHUNK = 64
PAIR = 2 * HEAD
SEG = 256
VMEM_LIMIT = 56 * 1024 * 1024

F32 = jnp.float32
BF16 = jnp.bfloat16


def _dot(a, b):
    return jnp.dot(a, b, preferred_element_type=F32)


def _dot_nt(a, b):
    return lax.dot_general(a, b, (((1,), (1,)), ((), ())), preferred_element_type=F32)


def _dot_tn(a, b):
    return lax.dot_general(a, b, (((0,), (0,)), ((), ())), preferred_element_type=F32)


def _split(x):
    hi = x.astype(BF16)
    lo = (x - hi.astype(F32)).astype(BF16)
    return hi, lo


def _seg_sum(x, bd):
    rows, n_groups = x.shape[0], x.shape[1] // SEG
    xb = x.astype(BF16)
    stacked = jnp.concatenate([xb[:, g * SEG:(g + 1) * SEG] for g in range(n_groups)], axis=0)
    out = _dot(stacked, bd)
    return jnp.concatenate([out[g * rows:(g + 1) * rows] for g in range(n_groups)], axis=1)


def _rms(x, g):
    return x * lax.rsqrt(jnp.mean(x * x, axis=-1, keepdims=True) + RMS_EPS) * g


def _sigmoid(x):
    return 0.5 * jnp.tanh(0.5 * x) + 0.5


def _cparams(*sem):
    return pltpu.CompilerParams(dimension_semantics=sem, vmem_limit_bytes=VMEM_LIMIT)


def _const_spec(shape):
    nd = len(shape)
    return pl.BlockSpec(shape, lambda *_: (0,) * nd, pipeline_mode=pl.Buffered(1))


def _halo_specs(tm, n_rows, width):
    nb = tm // POOL_HALO
    last_blk = n_rows // POOL_HALO - 1
    return [pl.BlockSpec((POOL_HALO, width), lambda i: (jnp.maximum(i * nb - 1, 0), 0)),
            pl.BlockSpec((POOL_HALO, width), lambda i: (jnp.minimum((i + 1) * nb, last_blk), 0))]


def _in_proj_kernel(seq, tm, x_ref, xprev_ref, xnext_ref, g_ref, w_ref, mup_ref, mun_ref, bgate_ref,
                    poolw_ref, pscale_ref, pooled_ref, zr_ref, gates_ref, ext_ref):
    pos0 = (pl.program_id(0) * tm) % seq
    at_start = pos0 == 0
    at_end = pos0 + tm == seq
    rows = lax.broadcasted_iota(jnp.int32, (tm, 1), 0)
    x_all = jnp.concatenate([x_ref[...], xprev_ref[...], xnext_ref[...]], axis=0)
    xn_all = _rms(x_all, g_ref[...]).astype(BF16)
    xn = xn_all[:tm]

    zp_all = _dot(xn_all, w_ref[:, 0:POOL_WIDTH])
    ext_ref[0:POOL_HALO, :] = jnp.where(at_start, 0.0, zp_all[tm:tm + POOL_HALO])
    ext_ref[POOL_HALO:POOL_HALO + tm, :] = zp_all[:tm]
    ext_ref[POOL_HALO + tm:, :] = jnp.where(at_end, 0.0, zp_all[tm + POOL_HALO:])

    gates_ref[...] = _sigmoid(_dot(xn, w_ref[:, POOL_WIDTH + RW_COLS:]) + bgate_ref[...]).astype(BF16)

    for c0 in range(0, RW_COLS, D):
        c1 = min(c0 + D, RW_COLS)
        z_all = _dot(xn_all, w_ref[:, POOL_WIDTH + c0:POOL_WIDTH + c1])
        z = z_all[:tm]
        prev_row = jnp.where(at_start, 0.0, z_all[tm + POOL_HALO - 1:tm + POOL_HALO])
        next_row = jnp.where(at_end, 0.0, z_all[tm + POOL_HALO:tm + POOL_HALO + 1])
        z_prev = jnp.where(rows == 0, prev_row, pltpu.roll(z, 1, axis=0))
        z_next = jnp.where(rows == tm - 1, next_row, pltpu.roll(z, tm - 1, axis=0))
        zs = z + mup_ref[:, c0:c1] * (z_prev - z) + mun_ref[:, c0:c1] * (z_next - z)
        zr_ref[:, c0:c1] = zs.astype(BF16)

    pos = pos0 + rows
    pooled = []
    for g, w in enumerate(POOL_WINDOWS):
        cs = slice(g * POOL_GROUP, (g + 1) * POOL_GROUP)
        acc = ext_ref[POOL_HALO - w // 2:POOL_HALO - w // 2 + tm, cs]
        for j in range(1 - w // 2, w // 2):
            acc = acc + ext_ref[POOL_HALO + j:POOL_HALO + j + tm, cs]
        cnt = jnp.minimum(pos + (w // 2 - 1), seq - 1) - jnp.maximum(pos - w // 2, 0) + 1
        pg = acc * (1.0 / cnt.astype(F32)) - ext_ref[POOL_HALO:POOL_HALO + tm, cs]
        pooled.append(_dot(pg.astype(BF16), poolw_ref[g]))
    pooled_ref[...] = (jnp.concatenate(pooled, axis=1) * pscale_ref[...]).astype(BF16)


def _in_proj(x, seq, tm, p):
    T = x.shape[0]
    assert seq % tm == 0 and T % seq == 0, "a row tile must not straddle two sequences"
    row = lambda i: (i, 0)
    consts = [p["g_mix"], p["w_in"], p["mu_prev"], p["mu_next"], p["b_gate"], p["pool_w"], p["pool_scale"]]
    return pl.pallas_call(
        functools.partial(_in_proj_kernel, seq, tm),
        grid=(T // tm,),
        in_specs=[pl.BlockSpec((tm, D), row)] + _halo_specs(tm, T, D) + [_const_spec(c.shape) for c in consts],
        out_specs=[pl.BlockSpec((tm, POOL_WIDTH), row), pl.BlockSpec((tm, RW_COLS), row),
                   pl.BlockSpec((tm, GATE_COLS), row)],
        out_shape=[jax.ShapeDtypeStruct((T, POOL_WIDTH), BF16), jax.ShapeDtypeStruct((T, RW_COLS), BF16),
                   jax.ShapeDtypeStruct((T, GATE_COLS), BF16)],
        scratch_shapes=[pltpu.VMEM((tm + 2 * POOL_HALO, POOL_WIDTH), F32)],
        compiler_params=_cparams("parallel"),
        name="in_proj",
    )(x, x, x, *consts)


def _bd(x, lo):
    zero = jnp.zeros_like(x)
    return jnp.concatenate([jnp.where(lo, x, zero), jnp.where(lo, zero, x)], axis=0)


def _undiag(full, lo):
    return jnp.where(lo, full[:CHUNK], full[CHUNK:])


def _wkv_kernel(n_chunks, zf_ref, zb_ref, kk_ref, ka_ref, rk_ref, w0_ref, a0_ref, wup_ref, aup_ref,
                bd_ref, trif_ref, trib_ref, yf_ref, yb_ref, bonus_ref, s_ref):
    @pl.when(pl.program_id(1) == 0)
    def _():
        s_ref[...] = jnp.zeros_like(s_ref)

    bd = bd_ref[...]
    k_k, k_a = kk_ref[...], ka_ref[...]
    y_refs = (yf_ref, yb_ref)
    n_pairs = D // PAIR
    dir_cols = lambda d: slice(d * D, (d + 1) * D)
    lora_cols = slice(3 * D, 3 * D + LORA_COLS)

    neg_k_k, neg_k_a, one_minus_k_a = -k_k, -k_a, 1.0 - k_a
    half_log2_decay = -0.5 * math.exp(-0.5) * math.log2(math.e)

    def neg_learning_rate(lora_in, d):
        return jnp.tanh(a0_ref[:, dir_cols(d)] + _dot(lora_in, aup_ref[:, dir_cols(d)])) * -0.5 - 0.5

    def decayed_key(k, neg_a):
        return k * (one_minus_k_a + neg_a * neg_k_a)

    def chunk_operands(d, z_ref, tri_ref):
        r = z_ref[:, 0:D].astype(F32)
        k = z_ref[:, D:2 * D].astype(F32)
        lora_in = z_ref[:, lora_cols]
        neg_kkr = k * neg_k_k
        neg_kk = neg_kkr * jnp.minimum(lax.rsqrt(_seg_sum(neg_kkr * neg_kkr, bd)), 1.0 / L2_EPS)
        half_w_raw = w0_ref[:, dir_cols(d)] + _dot(jnp.tanh(lora_in.astype(F32)).astype(BF16),
                                                   wup_ref[:, dir_cols(d)])
        lw = jnp.tanh(half_w_raw) * half_log2_decay + half_log2_decay
        neg_a = neg_learning_rate(lora_in, d)
        kd = decayed_key(k, neg_a)
        tri = tri_ref[...]
        cum = _dot(jnp.concatenate([tri, tri], axis=1), jnp.concatenate(_split(lw), axis=0))
        e_neg = jnp.exp2(-cum)
        last = CHUNK - 1 if d == 0 else 0
        ops = dict(rt=(r * jnp.exp2(cum)).astype(BF16), bt=(neg_kk * neg_a * e_neg).astype(BF16),
                   kt=(kd * e_neg).astype(BF16), at=(neg_kk * jnp.exp2(cum - lw)).astype(BF16),
                   wc=[jnp.exp2(cum[c * CHUNK + last:c * CHUNK + last + 1, :]) for c in range(n_chunks)])
        return ops, r, k, kd

    ops_f, r_f, k_f, kd_f = chunk_operands(0, zf_ref, trif_ref)
    ops_b, _, _, _ = chunk_operands(1, zb_ref, trib_ref)
    ops_f["v"] = zf_ref[:, 2 * D:3 * D]
    ops_b["v"] = zb_ref[:, 2 * D:3 * D]
    ops = (ops_f, ops_b)
    kd_b_here = decayed_key(k_f, neg_learning_rate(zf_ref[:, lora_cols], 1))
    bonus_ref[...] = _seg_sum(r_f * (kd_f + kd_b_here) * rk_ref[...], bd) * ops_f["v"].astype(F32)

    row = lax.broadcasted_iota(jnp.int32, (CHUNK, PAIR), 0)
    lane = lax.broadcasted_iota(jnp.int32, (CHUNK, PAIR), 1)
    lo = lane < HEAD
    col = jnp.where(lo, lane, lane - HEAD)
    eye = (row == col).astype(F32)
    insts = [(d, c, p) for d in range(2) for c in range(n_chunks) for p in range(n_pairs)]
    lanes = lambda p: slice(p * PAIR, (p + 1) * PAIR)
    rows_of = lambda c: slice(c * CHUNK, (c + 1) * CHUNK)
    get = lambda d, c, p, name: ops[d][name][rows_of(c), lanes(p)]

    xs, ts, a_ak, a_rb, a_rk = [], [], [], [], []
    for d, c, p in insts:
        strict = (row > col) if d == 0 else (row < col)
        incl = (row >= col) if d == 0 else (row <= col)
        lhs = jnp.concatenate([get(d, c, p, "at"), get(d, c, p, "rt")], axis=0)
        rhs = jnp.concatenate([_bd(get(d, c, p, "bt"), lo), _bd(get(d, c, p, "kt"), lo)], axis=0)
        out = _dot_nt(lhs, rhs)
        x = jnp.where(strict, out[:CHUNK, :PAIR], 0.0)
        xs.append(x)
        ts.append(eye + x)
        a_ak.append(jnp.where(strict, out[:CHUNK, PAIR:], 0.0).astype(BF16))
        a_rb.append(jnp.where(incl, out[CHUNK:, :PAIR], 0.0).astype(BF16))
        a_rk.append(jnp.where(incl, out[CHUNK:, PAIR:], 0.0).astype(BF16))

    n_steps = int(math.log2(CHUNK))
    for j in range(n_steps):
        for i in range(len(insts)):
            xb = xs[i].astype(BF16)
            rhs = _bd(xb, lo)
            if j == 0:
                xs[i] = _dot(xb, rhs)
            elif j < n_steps - 1:
                out = _dot(jnp.concatenate([xb, ts[i].astype(BF16)], axis=0), rhs)
                xs[i] = out[:CHUNK]
                ts[i] = ts[i] + out[CHUNK:]
            else:
                ts[i] = ts[i] + _dot(ts[i].astype(BF16), rhs)

    akv, arkv = [], []
    for i, (d, c, p) in enumerate(insts):
        out = _dot(jnp.concatenate([a_ak[i], a_rk[i]], axis=0), _bd(get(d, c, p, "v"), lo))
        akv.append(out[:CHUNK].astype(BF16))
        arkv.append(out[CHUNK:])

    ps, qs = [], []
    for i, (d, c, p) in enumerate(insts):
        rhs = jnp.concatenate([_bd(get(d, c, p, "at"), lo), _bd(akv[i], lo)], axis=1)
        pq = _dot(ts[i].astype(BF16), rhs)
        ps.append(pq[:, :PAIR].astype(BF16))
        qs.append(pq[:, PAIR:].astype(BF16))

    rps, y0s = [], []
    for i, (d, c, p) in enumerate(insts):
        ry = _dot(a_rb[i], jnp.concatenate([_bd(ps[i], lo), _bd(qs[i], lo)], axis=1))
        rps.append((get(d, c, p, "rt").astype(F32) + ry[:, :PAIR]).astype(BF16))
        y0s.append(ry[:, PAIR:] + arkv[i])

    ms, ns = [], []
    for i, (d, c, p) in enumerate(insts):
        wc = ops[d]["wc"][c][:, lanes(p)]
        b_end = (get(d, c, p, "bt").astype(F32) * wc).astype(BF16)
        k_end = (get(d, c, p, "kt").astype(F32) * wc).astype(BF16)
        lhs_t = jnp.concatenate([b_end, k_end], axis=0)
        v = get(d, c, p, "v")
        rhs = jnp.concatenate([jnp.concatenate([ps[i], qs[i]], axis=1),
                               jnp.concatenate([jnp.zeros_like(v), v], axis=1)], axis=0)
        full = _dot_tn(lhs_t, rhs)
        ms.append((eye * wc + _undiag(full[:, :PAIR], lo)).astype(BF16))
        ns.append(_undiag(full[:, PAIR:], lo))

    for step in range(n_chunks):
        for i, (d, c, p) in enumerate(insts):
            if c != (step if d == 0 else n_chunks - 1 - step):
                continue
            s0 = s_ref[d, :, lanes(p)]
            out = _dot(jnp.concatenate([ms[i], rps[i]], axis=0), _bd(s0.astype(BF16), lo))
            s_ref[d, :, lanes(p)] = out[:CHUNK] + ns[i]
            y_refs[d][rows_of(c), lanes(p)] = out[CHUNK:] + y0s[i]


def _wkv(zs, seq, n_chunks, p):
    T = zs.shape[0]
    tm = n_chunks * CHUNK
    assert seq % tm == 0 and T % seq == 0, "a block must not straddle two sequences"
    nc = seq // tm
    pos = (lambda b, c: (b * nc + c, 0), lambda b, c: (b * nc + nc - 1 - c, 0))
    params = [p["k_k"], p["k_a"], p["r_k"], p["w0"], p["a0"], p["w_up"], p["a_up"], p["bd"], p["tri_f"],
              p["tri_b"]]
    return pl.pallas_call(
        functools.partial(_wkv_kernel, n_chunks),
        grid=(T // seq, nc),
        in_specs=[pl.BlockSpec((tm, RW_COLS), pos[0]), pl.BlockSpec((tm, RW_COLS), pos[1])]
                 + [_const_spec(c.shape) for c in params],
        out_specs=[pl.BlockSpec((tm, D), pos[0]), pl.BlockSpec((tm, D), pos[1]),
                   pl.BlockSpec((tm, D), pos[0])],
        out_shape=[jax.ShapeDtypeStruct((T, D), F32)] * 3,
        scratch_shapes=[pltpu.VMEM((2, CHUNK, D), F32)],
        compiler_params=_cparams("parallel", "arbitrary"),
        name="wkv",
    )(zs, zs, *params)


def _out_kernel(yf_ref, yb_ref, bonus_ref, zgl_ref, pooled_ref, gates_ref, x_ref,
                lnw_ref, lnb_ref, gup_ref, wrw_ref, wpool_ref, wout_ref, bd_ref,
                gffn_ref, w1_ref, w2_ref, gfin_ref, o_ref):
    bd_mean = bd_ref[...]

    y = yf_ref[...] + yb_ref[...]
    yc = y - _seg_sum(y, bd_mean)
    var = _seg_sum(yc * yc, bd_mean)
    yn = yc * lax.rsqrt(var + GN_EPS) * lnw_ref[...] + lnb_ref[...] + bonus_ref[...]
    gate = _dot(_sigmoid(zgl_ref[...].astype(F32)).astype(BF16), gup_ref[...])
    rwkv_out = _dot((yn * gate).astype(BF16), wrw_ref[...])

    pool_out = _dot(pooled_ref[...], wpool_ref[...])
    gates = gates_ref[...].astype(F32)
    merged = gates[:, :D] * pool_out + gates[:, D:] * rwkv_out
    x = x_ref[...] + _dot(merged.astype(BF16), wout_ref[...])

    hn = _rms(x, gffn_ref[...]).astype(BF16)
    h = jnp.maximum(_dot(hn, w1_ref[...]), 0.0)
    x2 = x + _dot((h * h).astype(BF16), w2_ref[...])
    o_ref[...] = _rms(x2, gfin_ref[...])


def _out(y_f, y_b, bonus, zs, pooled, gates, x, tm, p):
    T = x.shape[0]
    assert T % tm == 0
    row = lambda i: (i, 0)
    consts = [p["ln_w"], p["ln_b"], p["g_up"], p["w_rwkv_br"], p["w_pool_br"], p["w_out"], p["bd_mean"],
              p["g_ffn"], p["w_ff1"], p["w_ff2"], p["g_final"]]
    gate_lora_blk = (3 * D + LORA_COLS) // GATE_LORA_COLS
    return pl.pallas_call(
        _out_kernel,
        grid=(T // tm,),
        in_specs=[pl.BlockSpec((tm, D), row), pl.BlockSpec((tm, D), row), pl.BlockSpec((tm, D), row),
                  pl.BlockSpec((tm, GATE_LORA_COLS), lambda i: (i, gate_lora_blk)),
                  pl.BlockSpec((tm, POOL_WIDTH), row), pl.BlockSpec((tm, GATE_COLS), row),
                  pl.BlockSpec((tm, D), row)]
                 + [_const_spec(c.shape) for c in consts],
        out_specs=pl.BlockSpec((tm, D), row),
        out_shape=jax.ShapeDtypeStruct((T, D), F32),
        compiler_params=_cparams("parallel"),
        name="out",
    )(y_f, y_b, bonus, zs, pooled, gates, x, *consts)


def _prepare_params(g_mix, w_in, b_gate, mu_prev, mu_next, pool_w, pool_scale, w_pool_br, k_k, k_a, r_k,
                    w0_f, w_up_f, a0_f, a_up_f, w0_b, w_up_b, a0_b, a_up_b, g_up, ln_w, ln_b, w_rwkv_br,
                    w_out, g_ffn, w_ff1, w_ff2, g_final):
    row = lambda a: a.reshape(1, -1).astype(F32)
    lora = w_up_f.shape[0]
    zeros = jnp.zeros((lora, 2 * D), F32)
    seg = np.arange(SEG) // HEAD
    t = np.arange(WKV_CHUNKS * CHUNK)
    same_chunk = (t[None, :] // CHUNK) == (t[:, None] // CHUNK)
    return {
        "g_mix": row(g_mix), "w_in": w_in.astype(BF16), "b_gate": row(b_gate),
        "mu_prev": row(mu_prev), "mu_next": row(mu_next),
        "pool_w": pool_w.astype(BF16), "pool_scale": row(pool_scale), "w_pool_br": w_pool_br.astype(BF16),
        "k_k": row(k_k), "k_a": row(k_a), "r_k": row(r_k),
        "w0": 0.5 * jnp.concatenate([row(w0_f), row(w0_b)], axis=1),
        "a0": 0.5 * jnp.concatenate([row(a0_f), row(a0_b)], axis=1),
        "w_up": (0.5 * jnp.concatenate([jnp.concatenate([w_up_f, w_up_b], axis=1), zeros], axis=0)).astype(BF16),
        "a_up": (0.5 * jnp.concatenate([zeros, jnp.concatenate([a_up_f, a_up_b], axis=1)], axis=0)).astype(BF16),
        "g_up": g_up.astype(BF16), "ln_w": row(ln_w), "ln_b": row(ln_b),
        "w_rwkv_br": w_rwkv_br.astype(BF16), "w_out": w_out.astype(BF16),
        "g_ffn": row(g_ffn), "w_ff1": w_ff1.astype(BF16), "w_ff2": w_ff2.astype(BF16), "g_final": row(g_final),
        "bd": jnp.asarray(seg[:, None] == seg[None, :], BF16),
        "bd_mean": jnp.asarray((seg[:, None] == seg[None, :]) / HEAD, BF16),
        "tri_f": jnp.asarray(same_chunk & (t[None, :] <= t[:, None]), BF16),
        "tri_b": jnp.asarray(same_chunk & (t[None, :] >= t[:, None]), BF16),
    }


TM_PROJ = 1024
WKV_CHUNKS = 4
TM_OUT = 512


def _trunk(x, p):
    batch, seq, _ = x.shape
    xf = x.reshape(batch * seq, D)
    pooled, zs, gates = _in_proj(xf, seq, TM_PROJ, p)
    y_f, y_b, bonus = _wkv(zs, seq, WKV_CHUNKS, p)
    out = _out(y_f, y_b, bonus, zs, pooled, gates, xf, TM_OUT, p)
    return out.reshape(batch, seq, D)


def kernel(x_prompt, x_sample, g_mix, w_in, b_gate, mu_prev, mu_next, pool_w, pool_scale, w_pool_br, k_k, k_a, r_k, w0_f, w_up_f, a0_f, a_up_f, w0_b, w_up_b, a0_b, a_up_b, g_up, ln_w, ln_b, w_rwkv_br, w_out, g_ffn, w_ff1, w_ff2, g_final):
    depth = g_mix.shape[0]
    layers = [_prepare_params(g_mix[l], w_in[l], b_gate[l], mu_prev[l], mu_next[l], pool_w[l], pool_scale[l],
                              w_pool_br[l], k_k[l], k_a[l], r_k[l], w0_f[l], w_up_f[l], a0_f[l], a_up_f[l],
                              w0_b[l], w_up_b[l], a0_b[l], a_up_b[l], g_up[l], ln_w[l], ln_b[l], w_rwkv_br[l],
                              w_out[l], g_ffn[l], w_ff1[l], w_ff2[l], g_final) for l in range(depth)]
    assert depth == 1, "the final norm is fused into the last layer's ffn; only depth 1 is supported"
    return tuple(_trunk(x, layers[0]) for x in (x_prompt, x_sample))
```

```python
import functools
import math

import jax
import jax.numpy as jnp
import numpy as np
from jax import lax
from jax.experimental import pallas as pl
from jax.experimental.pallas import tpu as pltpu

D = 1024
HEAD = 64
POOL_WIDTH = 512
POOL_GROUP = 128
POOL_WINDOWS = (2, 4, 8, 16)
POOL_HALO = 8
LORA_COLS = 128
GATE_LORA_COLS = 128
RW_COLS = 3 * D + LORA_COLS + GATE_LORA_COLS
GATE_COLS = 2 * D
RMS_EPS = 1e-6
GN_EPS = 64e-5
L2_EPS = 1e-12
CHUNK = 64
PAIR = 2 * HEAD
INV_BASE = 4
SEG = 256
VMEM_LIMIT = 56 * 1024 * 1024

F32 = jnp.float32
BF16 = jnp.bfloat16


def _dot(a, b):
    return jnp.dot(a, b, preferred_element_type=F32)


def _dot_nt(a, b):
    return lax.dot_general(a, b, (((1,), (1,)), ((), ())), preferred_element_type=F32)


def _dot_tn(a, b):
    return lax.dot_general(a, b, (((0,), (0,)), ((), ())), preferred_element_type=F32)


def _split(x):
    hi = x.astype(BF16)
    lo = (x - hi.astype(F32)).astype(BF16)
    return hi, lo


def _seg_sum(x, bd):
    rows, n_groups = x.shape[0], x.shape[1] // SEG
    xb = x.astype(BF16)
    stacked = jnp.concatenate([xb[:, g * SEG:(g + 1) * SEG] for g in range(n_groups)], axis=0)
    out = _dot(stacked, bd)
    return jnp.concatenate([out[g * rows:(g + 1) * rows] for g in range(n_groups)], axis=1)


def _rms(x, g):
    return x * lax.rsqrt(jnp.mean(x * x, axis=-1, keepdims=True) + RMS_EPS) * g


def _sigmoid(x):
    return 0.5 * jnp.tanh(0.5 * x) + 0.5


def _cparams(*sem):
    return pltpu.CompilerParams(dimension_semantics=sem, vmem_limit_bytes=VMEM_LIMIT)


def _const_spec(shape):
    nd = len(shape)
    return pl.BlockSpec(shape, lambda *_: (0,) * nd, pipeline_mode=pl.Buffered(1))


def _halo_specs(tm, n_rows, width):
    nb = tm // POOL_HALO
    last_blk = n_rows // POOL_HALO - 1
    return [pl.BlockSpec((POOL_HALO, width), lambda i: (jnp.maximum(i * nb - 1, 0), 0)),
            pl.BlockSpec((POOL_HALO, width), lambda i: (jnp.minimum((i + 1) * nb, last_blk), 0))]


def _in_proj_kernel(seq, tm, x_ref, xprev_ref, xnext_ref, g_ref, w_ref, mup_ref, mun_ref, bgate_ref,
                    poolw_ref, pscale_ref, pooled_ref, zr_ref, gates_ref, ext_ref):
    pos0 = (pl.program_id(0) * tm) % seq
    at_start = pos0 == 0
    at_end = pos0 + tm == seq
    rows = lax.broadcasted_iota(jnp.int32, (tm, 1), 0)
    x_all = jnp.concatenate([x_ref[...], xprev_ref[...], xnext_ref[...]], axis=0)
    xn_all = _rms(x_all, g_ref[...]).astype(BF16)
    xn = xn_all[:tm]

    zp_all = _dot(xn_all, w_ref[:, 0:POOL_WIDTH])
    ext_ref[0:POOL_HALO, :] = jnp.where(at_start, 0.0, zp_all[tm:tm + POOL_HALO])
    ext_ref[POOL_HALO:POOL_HALO + tm, :] = zp_all[:tm]
    ext_ref[POOL_HALO + tm:, :] = jnp.where(at_end, 0.0, zp_all[tm + POOL_HALO:])

    gates_ref[...] = _sigmoid(_dot(xn, w_ref[:, POOL_WIDTH + RW_COLS:]) + bgate_ref[...]).astype(BF16)

    for c0 in range(0, RW_COLS, D):
        c1 = min(c0 + D, RW_COLS)
        z_all = _dot(xn_all, w_ref[:, POOL_WIDTH + c0:POOL_WIDTH + c1])
        z = z_all[:tm]
        prev_row = jnp.where(at_start, 0.0, z_all[tm + POOL_HALO - 1:tm + POOL_HALO])
        next_row = jnp.where(at_end, 0.0, z_all[tm + POOL_HALO:tm + POOL_HALO + 1])
        z_prev = jnp.where(rows == 0, prev_row, pltpu.roll(z, 1, axis=0))
        z_next = jnp.where(rows == tm - 1, next_row, pltpu.roll(z, tm - 1, axis=0))
        zs = z + mup_ref[:, c0:c1] * (z_prev - z) + mun_ref[:, c0:c1] * (z_next - z)
        zr_ref[:, c0:c1] = zs.astype(BF16)

    pos = pos0 + rows
    pooled = []
    for g, w in enumerate(POOL_WINDOWS):
        cs = slice(g * POOL_GROUP, (g + 1) * POOL_GROUP)
        acc = ext_ref[POOL_HALO - w // 2:POOL_HALO - w // 2 + tm, cs]
        for j in range(1 - w // 2, w // 2):
            acc = acc + ext_ref[POOL_HALO + j:POOL_HALO + j + tm, cs]
        cnt = jnp.minimum(pos + (w // 2 - 1), seq - 1) - jnp.maximum(pos - w // 2, 0) + 1
        pg = acc * (1.0 / cnt.astype(F32)) - ext_ref[POOL_HALO:POOL_HALO + tm, cs]
        pooled.append(_dot(pg.astype(BF16), poolw_ref[g]))
    pooled_ref[...] = (jnp.concatenate(pooled, axis=1) * pscale_ref[...]).astype(BF16)


def _in_proj(x, seq, tm, p):
    T = x.shape[0]
    assert seq % tm == 0 and T % seq == 0, "a row tile must not straddle two sequences"
    row = lambda i: (i, 0)
    consts = [p["g_mix"], p["w_in"], p["mu_prev"], p["mu_next"], p["b_gate"], p["pool_w"], p["pool_scale"]]
    return pl.pallas_call(
        functools.partial(_in_proj_kernel, seq, tm),
        grid=(T // tm,),
        in_specs=[pl.BlockSpec((tm, D), row)] + _halo_specs(tm, T, D) + [_const_spec(c.shape) for c in consts],
        out_specs=[pl.BlockSpec((tm, POOL_WIDTH), row), pl.BlockSpec((tm, RW_COLS), row),
                   pl.BlockSpec((tm, GATE_COLS), row)],
        out_shape=[jax.ShapeDtypeStruct((T, POOL_WIDTH), BF16), jax.ShapeDtypeStruct((T, RW_COLS), BF16),
                   jax.ShapeDtypeStruct((T, GATE_COLS), BF16)],
        scratch_shapes=[pltpu.VMEM((tm + 2 * POOL_HALO, POOL_WIDTH), F32)],
        compiler_params=_cparams("parallel"),
        name="in_proj",
    )(x, x, x, *consts)


def _bd(x, lo):
    zero = jnp.zeros_like(x)
    return jnp.concatenate([jnp.where(lo, x, zero), jnp.where(lo, zero, x)], axis=0)


def _undiag(full, lo):
    return jnp.where(lo, full[:CHUNK], full[CHUNK:])


def _wkv_kernel(n_chunks, zf_ref, zb_ref, kk_ref, ka_ref, rk_ref, w0_ref, a0_ref, wup_ref, aup_ref,
                bd_ref, trif_ref, trib_ref, yf_ref, yb_ref, bonus_ref, s_ref):
    @pl.when(pl.program_id(1) == 0)
    def _():
        s_ref[...] = jnp.zeros_like(s_ref)

    bd = bd_ref[...]
    k_k, k_a = kk_ref[...], ka_ref[...]
    y_refs = (yf_ref, yb_ref)
    n_pairs = D // PAIR
    dir_cols = lambda d: slice(d * D, (d + 1) * D)
    lora_cols = slice(3 * D, 3 * D + LORA_COLS)

    neg_k_k, neg_k_a, one_minus_k_a = -k_k, -k_a, 1.0 - k_a
    half_log2_decay = -0.5 * math.exp(-0.5) * math.log2(math.e)

    def neg_learning_rate(lora_in, d):
        return jnp.tanh(a0_ref[:, dir_cols(d)] + _dot(lora_in, aup_ref[:, dir_cols(d)])) * -0.5 - 0.5

    def decayed_key(k, neg_a):
        return k * (one_minus_k_a + neg_a * neg_k_a)

    def chunk_operands(d, z_ref, tri_ref):
        r = z_ref[:, 0:D].astype(F32)
        k = z_ref[:, D:2 * D].astype(F32)
        lora_in = z_ref[:, lora_cols]
        neg_kkr = k * neg_k_k
        neg_kk = neg_kkr * jnp.minimum(lax.rsqrt(_seg_sum(neg_kkr * neg_kkr, bd)), 1.0 / L2_EPS)
        half_w_raw = w0_ref[:, dir_cols(d)] + _dot(jnp.tanh(lora_in.astype(F32)).astype(BF16),
                                                   wup_ref[:, dir_cols(d)])
        lw = jnp.tanh(half_w_raw) * half_log2_decay + half_log2_decay
        neg_a = neg_learning_rate(lora_in, d)
        kd = decayed_key(k, neg_a)
        tri = tri_ref[...]
        cum = _dot(jnp.concatenate([tri, tri], axis=1), jnp.concatenate(_split(lw), axis=0))
        e_neg = jnp.exp2(-cum)
        last = CHUNK - 1 if d == 0 else 0
        ops = dict(rt=(r * jnp.exp2(cum)).astype(BF16), bt=(neg_kk * neg_a * e_neg).astype(BF16),
                   kt=(kd * e_neg).astype(BF16), at=(neg_kk * jnp.exp2(cum - lw)).astype(BF16),
                   wc=[jnp.exp2(cum[c * CHUNK + last:c * CHUNK + last + 1, :]) for c in range(n_chunks)])
        return ops, r, k, kd

    ops_f, r_f, k_f, kd_f = chunk_operands(0, zf_ref, trif_ref)
    ops_b, _, _, _ = chunk_operands(1, zb_ref, trib_ref)
    ops_f["v"] = zf_ref[:, 2 * D:3 * D]
    ops_b["v"] = zb_ref[:, 2 * D:3 * D]
    ops = (ops_f, ops_b)
    kd_b_here = decayed_key(k_f, neg_learning_rate(zf_ref[:, lora_cols], 1))
    bonus_ref[...] = _seg_sum(r_f * (kd_f + kd_b_here) * rk_ref[...], bd) * ops_f["v"].astype(F32)

    row = lax.broadcasted_iota(jnp.int32, (CHUNK, PAIR), 0)
    lane = lax.broadcasted_iota(jnp.int32, (CHUNK, PAIR), 1)
    lo = lane < HEAD
    col = jnp.where(lo, lane, lane - HEAD)
    eye = (row == col).astype(F32)
    insts = [(d, c, p) for d in range(2) for c in range(n_chunks) for p in range(n_pairs)]
    lanes = lambda p: slice(p * PAIR, (p + 1) * PAIR)
    rows_of = lambda c: slice(c * CHUNK, (c + 1) * CHUNK)
    get = lambda d, c, p, name: ops[d][name][rows_of(c), lanes(p)]

    xs, ts, a_ak, a_rb, a_rk = [], [], [], [], []
    for d, c, p in insts:
        strict = (row > col) if d == 0 else (row < col)
        incl = (row >= col) if d == 0 else (row <= col)
        lhs = jnp.concatenate([get(d, c, p, "at"), get(d, c, p, "rt")], axis=0)
        rhs = jnp.concatenate([_bd(get(d, c, p, "bt"), lo), _bd(get(d, c, p, "kt"), lo)], axis=0)
        out = _dot_nt(lhs, rhs)
        xs.append(jnp.where(strict, out[:CHUNK, :PAIR], 0.0))
        a_ak.append(jnp.where(strict, out[:CHUNK, PAIR:], 0.0).astype(BF16))
        a_rb.append(jnp.where(incl, out[CHUNK:, :PAIR], 0.0).astype(BF16))
        a_rk.append(jnp.where(incl, out[CHUNK:, PAIR:], 0.0).astype(BF16))

    def same_block(size):
        return (row // size) == (col // size)

    in_base = same_block(INV_BASE)
    x_base = [jnp.where(in_base, x, 0.0).astype(BF16) for x in xs]
    ts = [eye + jnp.where(in_base, x, 0.0) for x in xs]
    x_sq = [_dot(xb, _bd(xb, lo)).astype(BF16) for xb in x_base]
    ts = [t + _dot(t.astype(BF16), _bd(x2, lo)) for t, x2 in zip(ts, x_sq)]
    size = INV_BASE
    while size < CHUNK:
        joins = same_block(2 * size) & ~same_block(size)
        ws = [_dot(jnp.where(joins, x, 0.0).astype(BF16), _bd(t.astype(BF16), lo)).astype(BF16)
              for x, t in zip(xs, ts)]
        ts = [t + _dot(t.astype(BF16), _bd(w, lo)) for t, w in zip(ts, ws)]
        size *= 2

    akv, arkv = [], []
    for i, (d, c, p) in enumerate(insts):
        out = _dot(jnp.concatenate([a_ak[i], a_rk[i]], axis=0), _bd(get(d, c, p, "v"), lo))
        akv.append(out[:CHUNK].astype(BF16))
        arkv.append(out[CHUNK:])

    ps, qs = [], []
    for i, (d, c, p) in enumerate(insts):
        rhs = jnp.concatenate([_bd(get(d, c, p, "at"), lo), _bd(akv[i], lo)], axis=1)
        pq = _dot(ts[i].astype(BF16), rhs)
        ps.append(pq[:, :PAIR].astype(BF16))
        qs.append(pq[:, PAIR:].astype(BF16))

    rps, y0s = [], []
    for i, (d, c, p) in enumerate(insts):
        ry = _dot(a_rb[i], jnp.concatenate([_bd(ps[i], lo), _bd(qs[i], lo)], axis=1))
        rps.append((get(d, c, p, "rt").astype(F32) + ry[:, :PAIR]).astype(BF16))
        y0s.append(ry[:, PAIR:] + arkv[i])

    ms, ns = [], []
    for i, (d, c, p) in enumerate(insts):
        wc = ops[d]["wc"][c][:, lanes(p)]
        b_end = (get(d, c, p, "bt").astype(F32) * wc).astype(BF16)
        k_end = (get(d, c, p, "kt").astype(F32) * wc).astype(BF16)
        lhs_t = jnp.concatenate([b_end, k_end], axis=0)
        v = get(d, c, p, "v")
        rhs = jnp.concatenate([jnp.concatenate([ps[i], qs[i]], axis=1),
                               jnp.concatenate([jnp.zeros_like(v), v], axis=1)], axis=0)
        full = _dot_tn(lhs_t, rhs)
        ms.append((eye * wc + _undiag(full[:, :PAIR], lo)).astype(BF16))
        ns.append(_undiag(full[:, PAIR:], lo))

    for step in range(n_chunks):
        for i, (d, c, p) in enumerate(insts):
            if c != (step if d == 0 else n_chunks - 1 - step):
                continue
            s0 = s_ref[d, :, lanes(p)]
            out = _dot(jnp.concatenate([ms[i], rps[i]], axis=0), _bd(s0.astype(BF16), lo))
            s_ref[d, :, lanes(p)] = out[:CHUNK] + ns[i]
            y_refs[d][rows_of(c), lanes(p)] = out[CHUNK:] + y0s[i]


def _wkv(zs, seq, n_chunks, p):
    T = zs.shape[0]
    tm = n_chunks * CHUNK
    assert seq % tm == 0 and T % seq == 0, "a block must not straddle two sequences"
    nc = seq // tm
    pos = (lambda b, c: (b * nc + c, 0), lambda b, c: (b * nc + nc - 1 - c, 0))
    params = [p["k_k"], p["k_a"], p["r_k"], p["w0"], p["a0"], p["w_up"], p["a_up"], p["bd"], p["tri_f"],
              p["tri_b"]]
    return pl.pallas_call(
        functools.partial(_wkv_kernel, n_chunks),
        grid=(T // seq, nc),
        in_specs=[pl.BlockSpec((tm, RW_COLS), pos[0]), pl.BlockSpec((tm, RW_COLS), pos[1])]
                 + [_const_spec(c.shape) for c in params],
        out_specs=[pl.BlockSpec((tm, D), pos[0]), pl.BlockSpec((tm, D), pos[1]),
                   pl.BlockSpec((tm, D), pos[0])],
        out_shape=[jax.ShapeDtypeStruct((T, D), F32)] * 3,
        scratch_shapes=[pltpu.VMEM((2, CHUNK, D), F32)],
        compiler_params=_cparams("parallel", "arbitrary"),
        name="wkv",
    )(zs, zs, *params)


def _out_kernel(yf_ref, yb_ref, bonus_ref, zgl_ref, pooled_ref, gates_ref, x_ref,
                lnw_ref, lnb_ref, gup_ref, wrw_ref, wpool_ref, wout_ref, bd_ref,
                gffn_ref, w1_ref, w2_ref, gfin_ref, o_ref):
    bd_mean = bd_ref[...]

    y = yf_ref[...] + yb_ref[...]
    yc = y - _seg_sum(y, bd_mean)
    var = _seg_sum(yc * yc, bd_mean)
    yn = yc * lax.rsqrt(var + GN_EPS) * lnw_ref[...] + lnb_ref[...] + bonus_ref[...]
    gate = _dot(_sigmoid(zgl_ref[...].astype(F32)).astype(BF16), gup_ref[...])
    rwkv_out = _dot((yn * gate).astype(BF16), wrw_ref[...])

    pool_out = _dot(pooled_ref[...], wpool_ref[...])
    gates = gates_ref[...].astype(F32)
    merged = gates[:, :D] * pool_out + gates[:, D:] * rwkv_out
    x = x_ref[...] + _dot(merged.astype(BF16), wout_ref[...])

    hn = _rms(x, gffn_ref[...]).astype(BF16)
    h = jnp.maximum(_dot(hn, w1_ref[...]), 0.0)
    x2 = x + _dot((h * h).astype(BF16), w2_ref[...])
    o_ref[...] = _rms(x2, gfin_ref[...])


def _out(y_f, y_b, bonus, zs, pooled, gates, x, tm, p):
    T = x.shape[0]
    assert T % tm == 0
    row = lambda i: (i, 0)
    consts = [p["ln_w"], p["ln_b"], p["g_up"], p["w_rwkv_br"], p["w_pool_br"], p["w_out"], p["bd_mean"],
              p["g_ffn"], p["w_ff1"], p["w_ff2"], p["g_final"]]
    gate_lora_blk = (3 * D + LORA_COLS) // GATE_LORA_COLS
    return pl.pallas_call(
        _out_kernel,
        grid=(T // tm,),
        in_specs=[pl.BlockSpec((tm, D), row), pl.BlockSpec((tm, D), row), pl.BlockSpec((tm, D), row),
                  pl.BlockSpec((tm, GATE_LORA_COLS), lambda i: (i, gate_lora_blk)),
                  pl.BlockSpec((tm, POOL_WIDTH), row), pl.BlockSpec((tm, GATE_COLS), row),
                  pl.BlockSpec((tm, D), row)]
                 + [_const_spec(c.shape) for c in consts],
        out_specs=pl.BlockSpec((tm, D), row),
        out_shape=jax.ShapeDtypeStruct((T, D), F32),
        compiler_params=_cparams("parallel"),
        name="out",
    )(y_f, y_b, bonus, zs, pooled, gates, x, *consts)


def _prepare_params(g_mix, w_in, b_gate, mu_prev, mu_next, pool_w, pool_scale, w_pool_br, k_k, k_a, r_k,
                    w0_f, w_up_f, a0_f, a_up_f, w0_b, w_up_b, a0_b, a_up_b, g_up, ln_w, ln_b, w_rwkv_br,
                    w_out, g_ffn, w_ff1, w_ff2, g_final):
    row = lambda a: a.reshape(1, -1).astype(F32)
    lora = w_up_f.shape[0]
    zeros = jnp.zeros((lora, 2 * D), F32)
    seg = np.arange(SEG) // HEAD
    t = np.arange(WKV_CHUNKS * CHUNK)
    same_chunk = (t[None, :] // CHUNK) == (t[:, None] // CHUNK)
    return {
        "g_mix": row(g_mix), "w_in": w_in.astype(BF16), "b_gate": row(b_gate),
        "mu_prev": row(mu_prev), "mu_next": row(mu_next),
        "pool_w": pool_w.astype(BF16), "pool_scale": row(pool_scale), "w_pool_br": w_pool_br.astype(BF16),
        "k_k": row(k_k), "k_a": row(k_a), "r_k": row(r_k),
        "w0": 0.5 * jnp.concatenate([row(w0_f), row(w0_b)], axis=1),
        "a0": 0.5 * jnp.concatenate([row(a0_f), row(a0_b)], axis=1),
        "w_up": (0.5 * jnp.concatenate([jnp.concatenate([w_up_f, w_up_b], axis=1), zeros], axis=0)).astype(BF16),
        "a_up": (0.5 * jnp.concatenate([zeros, jnp.concatenate([a_up_f, a_up_b], axis=1)], axis=0)).astype(BF16),
        "g_up": g_up.astype(BF16), "ln_w": row(ln_w), "ln_b": row(ln_b),
        "w_rwkv_br": w_rwkv_br.astype(BF16), "w_out": w_out.astype(BF16),
        "g_ffn": row(g_ffn), "w_ff1": w_ff1.astype(BF16), "w_ff2": w_ff2.astype(BF16), "g_final": row(g_final),
        "bd": jnp.asarray(seg[:, None] == seg[None, :], BF16),
        "bd_mean": jnp.asarray((seg[:, None] == seg[None, :]) / HEAD, BF16),
        "tri_f": jnp.asarray(same_chunk & (t[None, :] <= t[:, None]), BF16),
        "tri_b": jnp.asarray(same_chunk & (t[None, :] >= t[:, None]), BF16),
    }


TM_PROJ = 1024
WKV_CHUNKS = 4
TM_OUT = 512


def _trunk(x, p):
    batch, seq, _ = x.shape
    xf = x.reshape(batch * seq, D)
    pooled, zs, gates = _in_proj(xf, seq, TM_PROJ, p)
    y_f, y_b, bonus = _wkv(zs, seq, WKV_CHUNKS, p)
    out = _out(y_f, y_b, bonus, zs, pooled, gates, xf, TM_OUT, p)
    return out.reshape(batch, seq, D)


def kernel(x_prompt, x_sample, g_mix, w_in, b_gate, mu_prev, mu_next, pool_w, pool_scale, w_pool_br, k_k, k_a, r_k, w0_f, w_up_f, a0_f, a_up_f, w0_b, w_up_b, a0_b, a_up_b, g_up, ln_w, ln_b, w_rwkv_br, w_out, g_ffn, w_ff1, w_ff2, g_final):
    depth = g_mix.shape[0]
    layers = [_prepare_params(g_mix[l], w_in[l], b_gate[l], mu_prev[l], mu_next[l], pool_w[l], pool_scale[l],
                              w_pool_br[l], k_k[l], k_a[l], r_k[l], w0_f[l], w_up_f[l], a0_f[l], a_up_f[l],
                              w0_b[l], w_up_b[l], a0_b[l], a_up_b[l], g_up[l], ln_w[l], ln_b[l], w_rwkv_br[l],
                              w_out[l], g_ffn[l], w_ff1[l], w_ff2[l], g_final) for l in range(depth)]
    assert depth == 1, "the final norm is fused into the last layer's ffn; only depth 1 is supported"
    return tuple(_trunk(x, layers[0]) for x in (x_prompt, x_sample))
```

```python
import functools
import math

import jax
import jax.numpy as jnp
import numpy as np
from jax import lax
from jax.experimental import pallas as pl
from jax.experimental.pallas import tpu as pltpu

D = 1024
HEAD = 64
POOL_WIDTH = 512
POOL_GROUP = 128
POOL_WINDOWS = (2, 4, 8, 16)
POOL_HALO = 8
LORA_COLS = 128
GATE_LORA_COLS = 128
RW_COLS = 3 * D + LORA_COLS + GATE_LORA_COLS
GATE_COLS = 2 * D
RMS_EPS = 1e-6
GN_EPS = 64e-5
L2_EPS = 1e-12
CHUNK = 64
PAIR = 2 * HEAD
INV_BASE = 4
SEG = 256
VMEM_LIMIT = 56 * 1024 * 1024

F32 = jnp.float32
BF16 = jnp.bfloat16


def _dot(a, b):
    return jnp.dot(a, b, preferred_element_type=F32)


def _dot_nt(a, b):
    return lax.dot_general(a, b, (((1,), (1,)), ((), ())), preferred_element_type=F32)


def _dot_tn(a, b):
    return lax.dot_general(a, b, (((0,), (0,)), ((), ())), preferred_element_type=F32)


def _split(x):
    hi = x.astype(BF16)
    lo = (x - hi.astype(F32)).astype(BF16)
    return hi, lo


def _seg_sum(x, bd):
    rows, n_groups = x.shape[0], x.shape[1] // SEG
    xb = x.astype(BF16)
    stacked = jnp.concatenate([xb[:, g * SEG:(g + 1) * SEG] for g in range(n_groups)], axis=0)
    out = _dot(stacked, bd)
    return jnp.concatenate([out[g * rows:(g + 1) * rows] for g in range(n_groups)], axis=1)


def _rms(x, g):
    return x * lax.rsqrt(jnp.mean(x * x, axis=-1, keepdims=True) + RMS_EPS) * g


def _sigmoid(x):
    return 0.5 * jnp.tanh(0.5 * x) + 0.5


def _cparams(*sem):
    return pltpu.CompilerParams(dimension_semantics=sem, vmem_limit_bytes=VMEM_LIMIT)


def _const_spec(shape):
    nd = len(shape)
    return pl.BlockSpec(shape, lambda *_: (0,) * nd, pipeline_mode=pl.Buffered(1))


def _halo_specs(tm, n_rows, width):
    nb = tm // POOL_HALO
    last_blk = n_rows // POOL_HALO - 1
    return [pl.BlockSpec((POOL_HALO, width), lambda i: (jnp.maximum(i * nb - 1, 0), 0)),
            pl.BlockSpec((POOL_HALO, width), lambda i: (jnp.minimum((i + 1) * nb, last_blk), 0))]


def _in_proj_kernel(seq, tm, x_ref, xprev_ref, xnext_ref, g_ref, w_ref, mup_ref, mun_ref, bgate_ref,
                    poolw_ref, pscale_ref, pooled_ref, zr_ref, gates_ref, ext_ref):
    pos0 = (pl.program_id(0) * tm) % seq
    at_start = pos0 == 0
    at_end = pos0 + tm == seq
    rows = lax.broadcasted_iota(jnp.int32, (tm, 1), 0)
    x_all = jnp.concatenate([x_ref[...], xprev_ref[...], xnext_ref[...]], axis=0)
    xn_all = _rms(x_all, g_ref[...]).astype(BF16)
    xn = xn_all[:tm]

    zp_all = _dot(xn_all, w_ref[:, 0:POOL_WIDTH])
    ext_ref[0:POOL_HALO, :] = jnp.where(at_start, 0.0, zp_all[tm:tm + POOL_HALO])
    ext_ref[POOL_HALO:POOL_HALO + tm, :] = zp_all[:tm]
    ext_ref[POOL_HALO + tm:, :] = jnp.where(at_end, 0.0, zp_all[tm + POOL_HALO:])

    gates_ref[...] = _sigmoid(_dot(xn, w_ref[:, POOL_WIDTH + RW_COLS:]) + bgate_ref[...]).astype(BF16)

    for c0 in range(0, RW_COLS, D):
        c1 = min(c0 + D, RW_COLS)
        z_all = _dot(xn_all, w_ref[:, POOL_WIDTH + c0:POOL_WIDTH + c1])
        z = z_all[:tm]
        prev_row = jnp.where(at_start, 0.0, z_all[tm + POOL_HALO - 1:tm + POOL_HALO])
        next_row = jnp.where(at_end, 0.0, z_all[tm + POOL_HALO:tm + POOL_HALO + 1])
        z_prev = jnp.where(rows == 0, prev_row, pltpu.roll(z, 1, axis=0))
        z_next = jnp.where(rows == tm - 1, next_row, pltpu.roll(z, tm - 1, axis=0))
        zs = z + mup_ref[:, c0:c1] * (z_prev - z) + mun_ref[:, c0:c1] * (z_next - z)
        zr_ref[:, c0:c1] = zs.astype(BF16)

    pos = pos0 + rows
    pooled = []
    for g, w in enumerate(POOL_WINDOWS):
        cs = slice(g * POOL_GROUP, (g + 1) * POOL_GROUP)
        acc = ext_ref[POOL_HALO - w // 2:POOL_HALO - w // 2 + tm, cs]
        for j in range(1 - w // 2, w // 2):
            acc = acc + ext_ref[POOL_HALO + j:POOL_HALO + j + tm, cs]
        cnt = jnp.minimum(pos + (w // 2 - 1), seq - 1) - jnp.maximum(pos - w // 2, 0) + 1
        pg = acc * (1.0 / cnt.astype(F32)) - ext_ref[POOL_HALO:POOL_HALO + tm, cs]
        pooled.append(_dot(pg.astype(BF16), poolw_ref[g]))
    pooled_ref[...] = (jnp.concatenate(pooled, axis=1) * pscale_ref[...]).astype(BF16)


def _in_proj(x, seq, tm, p):
    T = x.shape[0]
    assert seq % tm == 0 and T % seq == 0, "a row tile must not straddle two sequences"
    row = lambda i: (i, 0)
    consts = [p["g_mix"], p["w_in"], p["mu_prev"], p["mu_next"], p["b_gate"], p["pool_w"], p["pool_scale"]]
    return pl.pallas_call(
        functools.partial(_in_proj_kernel, seq, tm),
        grid=(T // tm,),
        in_specs=[pl.BlockSpec((tm, D), row)] + _halo_specs(tm, T, D) + [_const_spec(c.shape) for c in consts],
        out_specs=[pl.BlockSpec((tm, POOL_WIDTH), row), pl.BlockSpec((tm, RW_COLS), row),
                   pl.BlockSpec((tm, GATE_COLS), row)],
        out_shape=[jax.ShapeDtypeStruct((T, POOL_WIDTH), BF16), jax.ShapeDtypeStruct((T, RW_COLS), BF16),
                   jax.ShapeDtypeStruct((T, GATE_COLS), BF16)],
        scratch_shapes=[pltpu.VMEM((tm + 2 * POOL_HALO, POOL_WIDTH), F32)],
        compiler_params=_cparams("parallel"),
        name="in_proj",
    )(x, x, x, *consts)


def _bd(x, lo):
    zero = jnp.zeros_like(x)
    return jnp.concatenate([jnp.where(lo, x, zero), jnp.where(lo, zero, x)], axis=0)


def _undiag(full, lo):
    return jnp.where(lo, full[:CHUNK], full[CHUNK:])


def _wkv_kernel(n_chunks, zf_ref, zb_ref, kk_ref, ka_ref, rk_ref, w0_ref, a0_ref, wup_ref, aup_ref,
                bd_ref, trif_ref, trib_ref, yf_ref, yb_ref, bonus_ref, s_ref):
    @pl.when(pl.program_id(1) == 0)
    def _():
        s_ref[...] = jnp.zeros_like(s_ref)

    bd = bd_ref[...]
    k_k, k_a = kk_ref[...], ka_ref[...]
    y_refs = (yf_ref, yb_ref)
    n_pairs = D // PAIR
    dir_cols = lambda d: slice(d * D, (d + 1) * D)
    lora_cols = slice(3 * D, 3 * D + LORA_COLS)

    neg_k_k, neg_k_a, one_minus_k_a = -k_k, -k_a, 1.0 - k_a
    half_log2_decay = -0.5 * math.exp(-0.5) * math.log2(math.e)

    def neg_learning_rate(lora_in, d):
        return jnp.tanh(a0_ref[:, dir_cols(d)] + _dot(lora_in, aup_ref[:, dir_cols(d)])) * -0.5 - 0.5

    def decayed_key(k, neg_a):
        return k * (one_minus_k_a + neg_a * neg_k_a)

    def chunk_operands(d, z_ref, tri_ref):
        r = z_ref[:, 0:D].astype(F32)
        k = z_ref[:, D:2 * D].astype(F32)
        lora_in = z_ref[:, lora_cols]
        neg_kkr = k * neg_k_k
        neg_kk = neg_kkr * jnp.minimum(lax.rsqrt(_seg_sum(neg_kkr * neg_kkr, bd)), 1.0 / L2_EPS)
        half_w_raw = w0_ref[:, dir_cols(d)] + _dot(jnp.tanh(lora_in.astype(F32)).astype(BF16),
                                                   wup_ref[:, dir_cols(d)])
        lw = jnp.tanh(half_w_raw) * half_log2_decay + half_log2_decay
        neg_a = neg_learning_rate(lora_in, d)
        kd = decayed_key(k, neg_a)
        tri = tri_ref[...]
        cum = _dot(jnp.concatenate([tri, tri], axis=1), jnp.concatenate(_split(lw), axis=0))
        e_neg = jnp.exp2(-cum)
        last = CHUNK - 1 if d == 0 else 0
        ops = dict(rt=(r * jnp.exp2(cum)).astype(BF16), bt=(neg_kk * neg_a * e_neg).astype(BF16),
                   kt=(kd * e_neg).astype(BF16), at=(neg_kk * jnp.exp2(cum - lw)).astype(BF16),
                   wc=[jnp.exp2(cum[c * CHUNK + last:c * CHUNK + last + 1, :]) for c in range(n_chunks)])
        return ops, r, k, kd

    ops_f, r_f, k_f, kd_f = chunk_operands(0, zf_ref, trif_ref)
    ops_b, _, _, _ = chunk_operands(1, zb_ref, trib_ref)
    ops_f["v"] = zf_ref[:, 2 * D:3 * D]
    ops_b["v"] = zb_ref[:, 2 * D:3 * D]
    ops = (ops_f, ops_b)
    kd_b_here = decayed_key(k_f, neg_learning_rate(zf_ref[:, lora_cols], 1))
    bonus_ref[...] = _seg_sum(r_f * (kd_f + kd_b_here) * rk_ref[...], bd) * ops_f["v"].astype(F32)

    row = lax.broadcasted_iota(jnp.int32, (CHUNK, PAIR), 0)
    lane = lax.broadcasted_iota(jnp.int32, (CHUNK, PAIR), 1)
    lo = lane < HEAD
    col = jnp.where(lo, lane, lane - HEAD)
    eye = (row == col).astype(F32)
    insts = [(d, c, p) for d in range(2) for c in range(n_chunks) for p in range(n_pairs)]
    lanes = lambda p: slice(p * PAIR, (p + 1) * PAIR)
    rows_of = lambda c: slice(c * CHUNK, (c + 1) * CHUNK)
    get = lambda d, c, p, name: ops[d][name][rows_of(c), lanes(p)]

    xs, ts, a_ak, a_rb, a_rk = [], [], [], [], []
    for d, c, p in insts:
        strict = (row > col) if d == 0 else (row < col)
        incl = (row >= col) if d == 0 else (row <= col)
        lhs = jnp.concatenate([get(d, c, p, "at"), get(d, c, p, "rt")], axis=0)
        rhs = jnp.concatenate([_bd(get(d, c, p, "bt"), lo), _bd(get(d, c, p, "kt"), lo)], axis=0)
        out = _dot_nt(lhs, rhs)
        xs.append(jnp.where(strict, out[:CHUNK, :PAIR], 0.0))
        a_ak.append(jnp.where(strict, out[:CHUNK, PAIR:], 0.0).astype(BF16))
        a_rb.append(jnp.where(incl, out[CHUNK:, :PAIR], 0.0).astype(BF16))
        a_rk.append(jnp.where(incl, out[CHUNK:, PAIR:], 0.0).astype(BF16))

    def same_block(size):
        return (row // size) == (col // size)

    in_base = same_block(INV_BASE)
    x_base = [jnp.where(in_base, x, 0.0).astype(BF16) for x in xs]
    ts = [eye + jnp.where(in_base, x, 0.0) for x in xs]
    x_sq = [_dot(xb, _bd(xb, lo)).astype(BF16) for xb in x_base]
    ts = [t + _dot(t.astype(BF16), _bd(x2, lo)) for t, x2 in zip(ts, x_sq)]
    def joins(size):
        return same_block(2 * size) & ~same_block(size)

    size = INV_BASE
    while size < CHUNK:
        pair_levels = 2 * size < CHUNK
        offs = [jnp.concatenate([jnp.where(joins(s), x, 0.0).astype(BF16)
                                 for s in ((size, 2 * size) if pair_levels else (size,))], axis=0) for x in xs]
        wy = [_dot(o, _bd(t.astype(BF16), lo)) for o, t in zip(offs, ts)]
        ws = [v[:CHUNK].astype(BF16) for v in wy]
        if pair_levels:
            upd = [_dot(jnp.concatenate([t.astype(BF16), v[CHUNK:].astype(BF16)], axis=0), _bd(w, lo))
                   for t, v, w in zip(ts, wy, ws)]
            ts = [t + u[:CHUNK] for t, u in zip(ts, upd)]
            ws = [(v[CHUNK:] + u[CHUNK:]).astype(BF16) for v, u in zip(wy, upd)]
            size *= 2
        ts = [t + _dot(t.astype(BF16), _bd(w, lo)) for t, w in zip(ts, ws)]
        size *= 2

    akv, arkv = [], []
    for i, (d, c, p) in enumerate(insts):
        out = _dot(jnp.concatenate([a_ak[i], a_rk[i]], axis=0), _bd(get(d, c, p, "v"), lo))
        akv.append(out[:CHUNK].astype(BF16))
        arkv.append(out[CHUNK:])

    ps, qs = [], []
    for i, (d, c, p) in enumerate(insts):
        rhs = jnp.concatenate([_bd(get(d, c, p, "at"), lo), _bd(akv[i], lo)], axis=1)
        pq = _dot(ts[i].astype(BF16), rhs)
        ps.append(pq[:, :PAIR].astype(BF16))
        qs.append(pq[:, PAIR:].astype(BF16))

    rps, y0s = [], []
    for i, (d, c, p) in enumerate(insts):
        ry = _dot(a_rb[i], jnp.concatenate([_bd(ps[i], lo), _bd(qs[i], lo)], axis=1))
        rps.append((get(d, c, p, "rt").astype(F32) + ry[:, :PAIR]).astype(BF16))
        y0s.append(ry[:, PAIR:] + arkv[i])

    ms, ns = [], []
    for i, (d, c, p) in enumerate(insts):
        wc = ops[d]["wc"][c][:, lanes(p)]
        b_end = (get(d, c, p, "bt").astype(F32) * wc).astype(BF16)
        k_end = (get(d, c, p, "kt").astype(F32) * wc).astype(BF16)
        lhs_t = jnp.concatenate([b_end, k_end], axis=0)
        v = get(d, c, p, "v")
        rhs = jnp.concatenate([jnp.concatenate([ps[i], qs[i]], axis=1),
                               jnp.concatenate([jnp.zeros_like(v), v], axis=1)], axis=0)
        full = _dot_tn(lhs_t, rhs)
        ms.append((eye * wc + _undiag(full[:, :PAIR], lo)).astype(BF16))
        ns.append(_undiag(full[:, PAIR:], lo))

    for step in range(n_chunks):
        for i, (d, c, p) in enumerate(insts):
            if c != (step if d == 0 else n_chunks - 1 - step):
                continue
            s0 = s_ref[d, :, lanes(p)]
            out = _dot(jnp.concatenate([ms[i], rps[i]], axis=0), _bd(s0.astype(BF16), lo))
            s_ref[d, :, lanes(p)] = out[:CHUNK] + ns[i]
            y_refs[d][rows_of(c), lanes(p)] = out[CHUNK:] + y0s[i]


def _wkv(zs, seq, n_chunks, p):
    T = zs.shape[0]
    tm = n_chunks * CHUNK
    assert seq % tm == 0 and T % seq == 0, "a block must not straddle two sequences"
    nc = seq // tm
    pos = (lambda b, c: (b * nc + c, 0), lambda b, c: (b * nc + nc - 1 - c, 0))
    params = [p["k_k"], p["k_a"], p["r_k"], p["w0"], p["a0"], p["w_up"], p["a_up"], p["bd"], p["tri_f"],
              p["tri_b"]]
    return pl.pallas_call(
        functools.partial(_wkv_kernel, n_chunks),
        grid=(T // seq, nc),
        in_specs=[pl.BlockSpec((tm, RW_COLS), pos[0]), pl.BlockSpec((tm, RW_COLS), pos[1])]
                 + [_const_spec(c.shape) for c in params],
        out_specs=[pl.BlockSpec((tm, D), pos[0]), pl.BlockSpec((tm, D), pos[1]),
                   pl.BlockSpec((tm, D), pos[0])],
        out_shape=[jax.ShapeDtypeStruct((T, D), F32)] * 3,
        scratch_shapes=[pltpu.VMEM((2, CHUNK, D), F32)],
        compiler_params=_cparams("parallel", "arbitrary"),
        name="wkv",
    )(zs, zs, *params)


def _out_kernel(yf_ref, yb_ref, bonus_ref, zgl_ref, pooled_ref, gates_ref, x_ref,
                lnw_ref, lnb_ref, gup_ref, wrw_ref, wpool_ref, wout_ref, bd_ref,
                gffn_ref, w1_ref, w2_ref, gfin_ref, o_ref):
    bd_mean = bd_ref[...]

    y = yf_ref[...] + yb_ref[...]
    yc = y - _seg_sum(y, bd_mean)
    var = _seg_sum(yc * yc, bd_mean)
    yn = yc * lax.rsqrt(var + GN_EPS) * lnw_ref[...] + lnb_ref[...] + bonus_ref[...]
    gate = _dot(_sigmoid(zgl_ref[...].astype(F32)).astype(BF16), gup_ref[...])
    rwkv_out = _dot((yn * gate).astype(BF16), wrw_ref[...])

    pool_out = _dot(pooled_ref[...], wpool_ref[...])
    gates = gates_ref[...].astype(F32)
    merged = gates[:, :D] * pool_out + gates[:, D:] * rwkv_out
    x = x_ref[...] + _dot(merged.astype(BF16), wout_ref[...])

    hn = _rms(x, gffn_ref[...]).astype(BF16)
    h = jnp.maximum(_dot(hn, w1_ref[...]), 0.0)
    x2 = x + _dot((h * h).astype(BF16), w2_ref[...])
    o_ref[...] = _rms(x2, gfin_ref[...])


def _out(y_f, y_b, bonus, zs, pooled, gates, x, tm, p):
    T = x.shape[0]
    assert T % tm == 0
    row = lambda i: (i, 0)
    consts = [p["ln_w"], p["ln_b"], p["g_up"], p["w_rwkv_br"], p["w_pool_br"], p["w_out"], p["bd_mean"],
              p["g_ffn"], p["w_ff1"], p["w_ff2"], p["g_final"]]
    gate_lora_blk = (3 * D + LORA_COLS) // GATE_LORA_COLS
    return pl.pallas_call(
        _out_kernel,
        grid=(T // tm,),
        in_specs=[pl.BlockSpec((tm, D), row), pl.BlockSpec((tm, D), row), pl.BlockSpec((tm, D), row),
                  pl.BlockSpec((tm, GATE_LORA_COLS), lambda i: (i, gate_lora_blk)),
                  pl.BlockSpec((tm, POOL_WIDTH), row), pl.BlockSpec((tm, GATE_COLS), row),
                  pl.BlockSpec((tm, D), row)]
                 + [_const_spec(c.shape) for c in consts],
        out_specs=pl.BlockSpec((tm, D), row),
        out_shape=jax.ShapeDtypeStruct((T, D), F32),
        compiler_params=_cparams("parallel"),
        name="out",
    )(y_f, y_b, bonus, zs, pooled, gates, x, *consts)


def _prepare_params(g_mix, w_in, b_gate, mu_prev, mu_next, pool_w, pool_scale, w_pool_br, k_k, k_a, r_k,
                    w0_f, w_up_f, a0_f, a_up_f, w0_b, w_up_b, a0_b, a_up_b, g_up, ln_w, ln_b, w_rwkv_br,
                    w_out, g_ffn, w_ff1, w_ff2, g_final):
    row = lambda a: a.reshape(1, -1).astype(F32)
    lora = w_up_f.shape[0]
    zeros = jnp.zeros((lora, 2 * D), F32)
    seg = np.arange(SEG) // HEAD
    t = np.arange(WKV_CHUNKS * CHUNK)
    same_chunk = (t[None, :] // CHUNK) == (t[:, None] // CHUNK)
    return {
        "g_mix": row(g_mix), "w_in": w_in.astype(BF16), "b_gate": row(b_gate),
        "mu_prev": row(mu_prev), "mu_next": row(mu_next),
        "pool_w": pool_w.astype(BF16), "pool_scale": row(pool_scale), "w_pool_br": w_pool_br.astype(BF16),
        "k_k": row(k_k), "k_a": row(k_a), "r_k": row(r_k),
        "w0": 0.5 * jnp.concatenate([row(w0_f), row(w0_b)], axis=1),
        "a0": 0.5 * jnp.concatenate([row(a0_f), row(a0_b)], axis=1),
        "w_up": (0.5 * jnp.concatenate([jnp.concatenate([w_up_f, w_up_b], axis=1), zeros], axis=0)).astype(BF16),
        "a_up": (0.5 * jnp.concatenate([zeros, jnp.concatenate([a_up_f, a_up_b], axis=1)], axis=0)).astype(BF16),
        "g_up": g_up.astype(BF16), "ln_w": row(ln_w), "ln_b": row(ln_b),
        "w_rwkv_br": w_rwkv_br.astype(BF16), "w_out": w_out.astype(BF16),
        "g_ffn": row(g_ffn), "w_ff1": w_ff1.astype(BF16), "w_ff2": w_ff2.astype(BF16), "g_final": row(g_final),
        "bd": jnp.asarray(seg[:, None] == seg[None, :], BF16),
        "bd_mean": jnp.asarray((seg[:, None] == seg[None, :]) / HEAD, BF16),
        "tri_f": jnp.asarray(same_chunk & (t[None, :] <= t[:, None]), BF16),
        "tri_b": jnp.asarray(same_chunk & (t[None, :] >= t[:, None]), BF16),
    }


TM_PROJ = 1024
WKV_CHUNKS = 4
TM_OUT = 512


def _trunk(x, p):
    batch, seq, _ = x.shape
    xf = x.reshape(batch * seq, D)
    pooled, zs, gates = _in_proj(xf, seq, TM_PROJ, p)
    y_f, y_b, bonus = _wkv(zs, seq, WKV_CHUNKS, p)
    out = _out(y_f, y_b, bonus, zs, pooled, gates, xf, TM_OUT, p)
    return out.reshape(batch, seq, D)


def kernel(x_prompt, x_sample, g_mix, w_in, b_gate, mu_prev, mu_next, pool_w, pool_scale, w_pool_br, k_k, k_a, r_k, w0_f, w_up_f, a0_f, a_up_f, w0_b, w_up_b, a0_b, a_up_b, g_up, ln_w, ln_b, w_rwkv_br, w_out, g_ffn, w_ff1, w_ff2, g_final):
    depth = g_mix.shape[0]
    layers = [_prepare_params(g_mix[l], w_in[l], b_gate[l], mu_prev[l], mu_next[l], pool_w[l], pool_scale[l],
                              w_pool_br[l], k_k[l], k_a[l], r_k[l], w0_f[l], w_up_f[l], a0_f[l], a_up_f[l],
                              w0_b[l], w_up_b[l], a0_b[l], a_up_b[l], g_up[l], ln_w[l], ln_b[l], w_rwkv_br[l],
                              w_out[l], g_ffn[l], w_ff1[l], w_ff2[l], g_final) for l in range(depth)]
    assert depth == 1, "the final norm is fused into the last layer's ffn; only depth 1 is supported"
    return tuple(_trunk(x, layers[0]) for x in (x_prompt, x_sample))
```

```python
import functools
import math

import jax
import jax.numpy as jnp
import numpy as np
from jax import lax
from jax.experimental import pallas as pl
from jax.experimental.pallas import tpu as pltpu

D = 1024
HEAD = 64
POOL_WIDTH = 512
POOL_GROUP = 128
POOL_WINDOWS = (2, 4, 8, 16)
POOL_HALO = 8
LORA_COLS = 128
GATE_LORA_COLS = 128
RW_COLS = 3 * D + LORA_COLS + GATE_LORA_COLS
GATE_COLS = 2 * D
RMS_EPS = 1e-6
GN_EPS = 64e-5
L2_EPS = 1e-12
CHUNK = 64
PAIR = 2 * HEAD
INV_BASE = 4
SEG = 256
VMEM_LIMIT = 56 * 1024 * 1024

F32 = jnp.float32
BF16 = jnp.bfloat16


def _dot(a, b):
    return jnp.dot(a, b, preferred_element_type=F32)


def _dot_nt(a, b):
    return lax.dot_general(a, b, (((1,), (1,)), ((), ())), preferred_element_type=F32)


def _dot_tn(a, b):
    return lax.dot_general(a, b, (((0,), (0,)), ((), ())), preferred_element_type=F32)


def _split(x):
    hi = x.astype(BF16)
    lo = (x - hi.astype(F32)).astype(BF16)
    return hi, lo


def _seg_sum(x, bd):
    rows, n_groups = x.shape[0], x.shape[1] // SEG
    xb = x.astype(BF16)
    stacked = jnp.concatenate([xb[:, g * SEG:(g + 1) * SEG] for g in range(n_groups)], axis=0)
    out = _dot(stacked, bd)
    return jnp.concatenate([out[g * rows:(g + 1) * rows] for g in range(n_groups)], axis=1)


def _rms(x, g):
    return x * lax.rsqrt(jnp.mean(x * x, axis=-1, keepdims=True) + RMS_EPS) * g


def _sigmoid(x):
    return 0.5 * jnp.tanh(0.5 * x) + 0.5


def _cparams(*sem):
    return pltpu.CompilerParams(dimension_semantics=sem, vmem_limit_bytes=VMEM_LIMIT)


def _const_spec(shape):
    nd = len(shape)
    return pl.BlockSpec(shape, lambda *_: (0,) * nd, pipeline_mode=pl.Buffered(1))


def _halo_specs(tm, n_rows, width):
    nb = tm // POOL_HALO
    last_blk = n_rows // POOL_HALO - 1
    return [pl.BlockSpec((POOL_HALO, width), lambda i: (jnp.maximum(i * nb - 1, 0), 0)),
            pl.BlockSpec((POOL_HALO, width), lambda i: (jnp.minimum((i + 1) * nb, last_blk), 0))]


def _in_proj_kernel(seq, tm, x_ref, xprev_ref, xnext_ref, g_ref, w_ref, mup_ref, mun_ref, bgate_ref,
                    poolw_ref, pscale_ref, pooled_ref, zr_ref, gates_ref, ext_ref):
    pos0 = (pl.program_id(0) * tm) % seq
    at_start = pos0 == 0
    at_end = pos0 + tm == seq
    rows = lax.broadcasted_iota(jnp.int32, (tm, 1), 0)
    x_all = jnp.concatenate([x_ref[...], xprev_ref[...], xnext_ref[...]], axis=0)
    xn_all = _rms(x_all, g_ref[...]).astype(BF16)
    xn = xn_all[:tm]

    zp_all = _dot(xn_all, w_ref[:, 0:POOL_WIDTH])
    ext_ref[0:POOL_HALO, :] = jnp.where(at_start, 0.0, zp_all[tm:tm + POOL_HALO])
    ext_ref[POOL_HALO:POOL_HALO + tm, :] = zp_all[:tm]
    ext_ref[POOL_HALO + tm:, :] = jnp.where(at_end, 0.0, zp_all[tm + POOL_HALO:])

    gates_ref[...] = _sigmoid(_dot(xn, w_ref[:, POOL_WIDTH + RW_COLS:]) + bgate_ref[...]).astype(BF16)

    for c0 in range(0, RW_COLS, D):
        c1 = min(c0 + D, RW_COLS)
        z_all = _dot(xn_all, w_ref[:, POOL_WIDTH + c0:POOL_WIDTH + c1])
        z = z_all[:tm]
        prev_row = jnp.where(at_start, 0.0, z_all[tm + POOL_HALO - 1:tm + POOL_HALO])
        next_row = jnp.where(at_end, 0.0, z_all[tm + POOL_HALO:tm + POOL_HALO + 1])
        z_prev = jnp.where(rows == 0, prev_row, pltpu.roll(z, 1, axis=0))
        z_next = jnp.where(rows == tm - 1, next_row, pltpu.roll(z, tm - 1, axis=0))
        zs = z + mup_ref[:, c0:c1] * (z_prev - z) + mun_ref[:, c0:c1] * (z_next - z)
        zr_ref[:, c0:c1] = zs.astype(BF16)

    pos = pos0 + rows
    pooled = []
    for g, w in enumerate(POOL_WINDOWS):
        cs = slice(g * POOL_GROUP, (g + 1) * POOL_GROUP)
        acc = ext_ref[POOL_HALO - w // 2:POOL_HALO - w // 2 + tm, cs]
        for j in range(1 - w // 2, w // 2):
            acc = acc + ext_ref[POOL_HALO + j:POOL_HALO + j + tm, cs]
        cnt = jnp.minimum(pos + (w // 2 - 1), seq - 1) - jnp.maximum(pos - w // 2, 0) + 1
        pg = acc * (1.0 / cnt.astype(F32)) - ext_ref[POOL_HALO:POOL_HALO + tm, cs]
        pooled.append(_dot(pg.astype(BF16), poolw_ref[g]))
    pooled_ref[...] = (jnp.concatenate(pooled, axis=1) * pscale_ref[...]).astype(BF16)


def _in_proj(x, seq, tm, p):
    T = x.shape[0]
    assert seq % tm == 0 and T % seq == 0, "a row tile must not straddle two sequences"
    row = lambda i: (i, 0)
    consts = [p["g_mix"], p["w_in"], p["mu_prev"], p["mu_next"], p["b_gate"], p["pool_w"], p["pool_scale"]]
    return pl.pallas_call(
        functools.partial(_in_proj_kernel, seq, tm),
        grid=(T // tm,),
        in_specs=[pl.BlockSpec((tm, D), row)] + _halo_specs(tm, T, D) + [_const_spec(c.shape) for c in consts],
        out_specs=[pl.BlockSpec((tm, POOL_WIDTH), row), pl.BlockSpec((tm, RW_COLS), row),
                   pl.BlockSpec((tm, GATE_COLS), row)],
        out_shape=[jax.ShapeDtypeStruct((T, POOL_WIDTH), BF16), jax.ShapeDtypeStruct((T, RW_COLS), BF16),
                   jax.ShapeDtypeStruct((T, GATE_COLS), BF16)],
        scratch_shapes=[pltpu.VMEM((tm + 2 * POOL_HALO, POOL_WIDTH), F32)],
        compiler_params=_cparams("parallel"),
        name="in_proj",
    )(x, x, x, *consts)


def _bd(x, lo):
    zero = jnp.zeros_like(x)
    return jnp.concatenate([jnp.where(lo, x, zero), jnp.where(lo, zero, x)], axis=0)


def _undiag(full, lo):
    return jnp.where(lo, full[:CHUNK], full[CHUNK:])


def _wkv_kernel(n_chunks, zf_ref, zb_ref, kk_ref, ka_ref, rk_ref, w0_ref, a0_ref, wup_ref, aup_ref,
                bd_ref, trif_ref, trib_ref, yf_ref, yb_ref, bonus_ref, s_ref):
    @pl.when(pl.program_id(1) == 0)
    def _():
        s_ref[...] = jnp.zeros_like(s_ref)

    bd = bd_ref[...]
    k_k, k_a = kk_ref[...], ka_ref[...]
    y_refs = (yf_ref, yb_ref)
    n_pairs = D // PAIR
    dir_cols = lambda d: slice(d * D, (d + 1) * D)
    lora_cols = slice(3 * D, 3 * D + LORA_COLS)

    neg_k_k, neg_k_a, one_minus_k_a = -k_k, -k_a, 1.0 - k_a
    half_log2_decay = -0.5 * math.exp(-0.5) * math.log2(math.e)

    def neg_learning_rate(lora_in, d):
        return jnp.tanh(a0_ref[:, dir_cols(d)] + _dot(lora_in, aup_ref[:, dir_cols(d)])) * -0.5 - 0.5

    def decayed_key(k, neg_a):
        return k * (one_minus_k_a + neg_a * neg_k_a)

    def chunk_operands(d, z_ref, tri_ref):
        r = z_ref[:, 0:D].astype(F32)
        k = z_ref[:, D:2 * D].astype(F32)
        lora_in = z_ref[:, lora_cols]
        neg_kkr = k * neg_k_k
        neg_kk = neg_kkr * jnp.minimum(lax.rsqrt(_seg_sum(neg_kkr * neg_kkr, bd)), 1.0 / L2_EPS)
        half_w_raw = w0_ref[:, dir_cols(d)] + _dot(jnp.tanh(lora_in.astype(F32)).astype(BF16),
                                                   wup_ref[:, dir_cols(d)])
        lw = jnp.tanh(half_w_raw) * half_log2_decay + half_log2_decay
        neg_a = neg_learning_rate(lora_in, d)
        kd = decayed_key(k, neg_a)
        tri = tri_ref[...]
        cum = _dot(jnp.concatenate([tri, tri], axis=1), jnp.concatenate(_split(lw), axis=0))
        e_pos = jnp.exp2(cum)
        e_neg = 1.0 / e_pos
        last = CHUNK - 1 if d == 0 else 0
        ops = dict(rt=(r * e_pos).astype(BF16), bt=(neg_kk * neg_a * e_neg).astype(BF16),
                   kt=(kd * e_neg).astype(BF16), at=(neg_kk * jnp.exp2(cum - lw)).astype(BF16),
                   wc=[jnp.exp2(cum[c * CHUNK + last:c * CHUNK + last + 1, :]) for c in range(n_chunks)])
        return ops, r, k, kd

    ops_f, r_f, k_f, kd_f = chunk_operands(0, zf_ref, trif_ref)
    ops_b, _, _, _ = chunk_operands(1, zb_ref, trib_ref)
    ops_f["v"] = zf_ref[:, 2 * D:3 * D]
    ops_b["v"] = zb_ref[:, 2 * D:3 * D]
    ops = (ops_f, ops_b)
    kd_b_here = decayed_key(k_f, neg_learning_rate(zf_ref[:, lora_cols], 1))
    bonus_ref[...] = _seg_sum(r_f * (kd_f + kd_b_here) * rk_ref[...], bd) * ops_f["v"].astype(F32)

    row = lax.broadcasted_iota(jnp.int32, (CHUNK, PAIR), 0)
    lane = lax.broadcasted_iota(jnp.int32, (CHUNK, PAIR), 1)
    lo = lane < HEAD
    col = jnp.where(lo, lane, lane - HEAD)
    eye = (row == col).astype(F32)
    insts = [(d, c, p) for d in range(2) for c in range(n_chunks) for p in range(n_pairs)]
    lanes = lambda p: slice(p * PAIR, (p + 1) * PAIR)
    rows_of = lambda c: slice(c * CHUNK, (c + 1) * CHUNK)
    get = lambda d, c, p, name: ops[d][name][rows_of(c), lanes(p)]

    xs, ts, a_ak, a_rb, a_rk = [], [], [], [], []
    for d, c, p in insts:
        strict = (row > col) if d == 0 else (row < col)
        incl = (row >= col) if d == 0 else (row <= col)
        lhs = jnp.concatenate([get(d, c, p, "at"), get(d, c, p, "rt")], axis=0)
        rhs = jnp.concatenate([_bd(get(d, c, p, "bt"), lo), _bd(get(d, c, p, "kt"), lo)], axis=0)
        out = _dot_nt(lhs, rhs)
        xs.append(jnp.where(strict, out[:CHUNK, :PAIR], 0.0))
        a_ak.append(jnp.where(strict, out[:CHUNK, PAIR:], 0.0).astype(BF16))
        a_rb.append(jnp.where(incl, out[CHUNK:, :PAIR], 0.0).astype(BF16))
        a_rk.append(jnp.where(incl, out[CHUNK:, PAIR:], 0.0).astype(BF16))

    def same_block(size):
        return (row // size) == (col // size)

    in_base = same_block(INV_BASE)
    x_base = [jnp.where(in_base, x, 0.0).astype(BF16) for x in xs]
    ts = [eye + jnp.where(in_base, x, 0.0) for x in xs]
    x_sq = [_dot(xb, _bd(xb, lo)).astype(BF16) for xb in x_base]
    ts = [t + _dot(t.astype(BF16), _bd(x2, lo)) for t, x2 in zip(ts, x_sq)]
    def joins(size):
        return same_block(2 * size) & ~same_block(size)

    size = INV_BASE
    while size < CHUNK:
        pair_levels = 2 * size < CHUNK
        offs = [jnp.concatenate([jnp.where(joins(s), x, 0.0).astype(BF16)
                                 for s in ((size, 2 * size) if pair_levels else (size,))], axis=0) for x in xs]
        wy = [_dot(o, _bd(t.astype(BF16), lo)) for o, t in zip(offs, ts)]
        ws = [v[:CHUNK].astype(BF16) for v in wy]
        if pair_levels:
            upd = [_dot(jnp.concatenate([t.astype(BF16), v[CHUNK:].astype(BF16)], axis=0), _bd(w, lo))
                   for t, v, w in zip(ts, wy, ws)]
            ts = [t + u[:CHUNK] for t, u in zip(ts, upd)]
            ws = [(v[CHUNK:] + u[CHUNK:]).astype(BF16) for v, u in zip(wy, upd)]
            size *= 2
        ts = [t + _dot(t.astype(BF16), _bd(w, lo)) for t, w in zip(ts, ws)]
        size *= 2

    akv, arkv = [], []
    for i, (d, c, p) in enumerate(insts):
        out = _dot(jnp.concatenate([a_ak[i], a_rk[i]], axis=0), _bd(get(d, c, p, "v"), lo))
        akv.append(out[:CHUNK].astype(BF16))
        arkv.append(out[CHUNK:])

    ps, qs = [], []
    for i, (d, c, p) in enumerate(insts):
        rhs = jnp.concatenate([_bd(get(d, c, p, "at"), lo), _bd(akv[i], lo)], axis=1)
        pq = _dot(ts[i].astype(BF16), rhs)
        ps.append(pq[:, :PAIR].astype(BF16))
        qs.append(pq[:, PAIR:].astype(BF16))

    rps, y0s = [], []
    for i, (d, c, p) in enumerate(insts):
        ry = _dot(a_rb[i], jnp.concatenate([_bd(ps[i], lo), _bd(qs[i], lo)], axis=1))
        rps.append((get(d, c, p, "rt").astype(F32) + ry[:, :PAIR]).astype(BF16))
        y0s.append(ry[:, PAIR:] + arkv[i])

    ms, ns = [], []
    for i, (d, c, p) in enumerate(insts):
        wc = ops[d]["wc"][c][:, lanes(p)]
        b_end = (get(d, c, p, "bt").astype(F32) * wc).astype(BF16)
        k_end = (get(d, c, p, "kt").astype(F32) * wc).astype(BF16)
        lhs_t = jnp.concatenate([b_end, k_end], axis=0)
        v = get(d, c, p, "v")
        rhs = jnp.concatenate([jnp.concatenate([ps[i], qs[i]], axis=1),
                               jnp.concatenate([jnp.zeros_like(v), v], axis=1)], axis=0)
        full = _dot_tn(lhs_t, rhs)
        ms.append((eye * wc + _undiag(full[:, :PAIR], lo)).astype(BF16))
        ns.append(_undiag(full[:, PAIR:], lo))

    for step in range(n_chunks):
        for i, (d, c, p) in enumerate(insts):
            if c != (step if d == 0 else n_chunks - 1 - step):
                continue
            s0 = s_ref[d, :, lanes(p)]
            out = _dot(jnp.concatenate([ms[i], rps[i]], axis=0), _bd(s0.astype(BF16), lo))
            s_ref[d, :, lanes(p)] = out[:CHUNK] + ns[i]
            y_refs[d][rows_of(c), lanes(p)] = out[CHUNK:] + y0s[i]


def _wkv(zs, seq, n_chunks, p):
    T = zs.shape[0]
    tm = n_chunks * CHUNK
    assert seq % tm == 0 and T % seq == 0, "a block must not straddle two sequences"
    nc = seq // tm
    pos = (lambda b, c: (b * nc + c, 0), lambda b, c: (b * nc + nc - 1 - c, 0))
    params = [p["k_k"], p["k_a"], p["r_k"], p["w0"], p["a0"], p["w_up"], p["a_up"], p["bd"], p["tri_f"],
              p["tri_b"]]
    return pl.pallas_call(
        functools.partial(_wkv_kernel, n_chunks),
        grid=(T // seq, nc),
        in_specs=[pl.BlockSpec((tm, RW_COLS), pos[0]), pl.BlockSpec((tm, RW_COLS), pos[1])]
                 + [_const_spec(c.shape) for c in params],
        out_specs=[pl.BlockSpec((tm, D), pos[0]), pl.BlockSpec((tm, D), pos[1]),
                   pl.BlockSpec((tm, D), pos[0])],
        out_shape=[jax.ShapeDtypeStruct((T, D), F32)] * 3,
        scratch_shapes=[pltpu.VMEM((2, CHUNK, D), F32)],
        compiler_params=_cparams("parallel", "arbitrary"),
        name="wkv",
    )(zs, zs, *params)


def _out_kernel(yf_ref, yb_ref, bonus_ref, zgl_ref, pooled_ref, gates_ref, x_ref,
                lnw_ref, lnb_ref, gup_ref, wrw_ref, wpool_ref, wout_ref, bd_ref,
                gffn_ref, w1_ref, w2_ref, gfin_ref, o_ref):
    bd_mean = bd_ref[...]

    y = yf_ref[...] + yb_ref[...]
    yc = y - _seg_sum(y, bd_mean)
    var = _seg_sum(yc * yc, bd_mean)
    yn = yc * lax.rsqrt(var + GN_EPS) * lnw_ref[...] + lnb_ref[...] + bonus_ref[...]
    gate = _dot(_sigmoid(zgl_ref[...].astype(F32)).astype(BF16), gup_ref[...])
    rwkv_out = _dot((yn * gate).astype(BF16), wrw_ref[...])

    pool_out = _dot(pooled_ref[...], wpool_ref[...])
    gates = gates_ref[...].astype(F32)
    merged = gates[:, :D] * pool_out + gates[:, D:] * rwkv_out
    x = x_ref[...] + _dot(merged.astype(BF16), wout_ref[...])

    hn = _rms(x, gffn_ref[...]).astype(BF16)
    h = jnp.maximum(_dot(hn, w1_ref[...]), 0.0)
    x2 = x + _dot((h * h).astype(BF16), w2_ref[...])
    o_ref[...] = _rms(x2, gfin_ref[...])


def _out(y_f, y_b, bonus, zs, pooled, gates, x, tm, p):
    T = x.shape[0]
    assert T % tm == 0
    row = lambda i: (i, 0)
    consts = [p["ln_w"], p["ln_b"], p["g_up"], p["w_rwkv_br"], p["w_pool_br"], p["w_out"], p["bd_mean"],
              p["g_ffn"], p["w_ff1"], p["w_ff2"], p["g_final"]]
    gate_lora_blk = (3 * D + LORA_COLS) // GATE_LORA_COLS
    return pl.pallas_call(
        _out_kernel,
        grid=(T // tm,),
        in_specs=[pl.BlockSpec((tm, D), row), pl.BlockSpec((tm, D), row), pl.BlockSpec((tm, D), row),
                  pl.BlockSpec((tm, GATE_LORA_COLS), lambda i: (i, gate_lora_blk)),
                  pl.BlockSpec((tm, POOL_WIDTH), row), pl.BlockSpec((tm, GATE_COLS), row),
                  pl.BlockSpec((tm, D), row)]
                 + [_const_spec(c.shape) for c in consts],
        out_specs=pl.BlockSpec((tm, D), row),
        out_shape=jax.ShapeDtypeStruct((T, D), F32),
        compiler_params=_cparams("parallel"),
        name="out",
    )(y_f, y_b, bonus, zs, pooled, gates, x, *consts)


def _prepare_params(g_mix, w_in, b_gate, mu_prev, mu_next, pool_w, pool_scale, w_pool_br, k_k, k_a, r_k,
                    w0_f, w_up_f, a0_f, a_up_f, w0_b, w_up_b, a0_b, a_up_b, g_up, ln_w, ln_b, w_rwkv_br,
                    w_out, g_ffn, w_ff1, w_ff2, g_final):
    row = lambda a: a.reshape(1, -1).astype(F32)
    lora = w_up_f.shape[0]
    zeros = jnp.zeros((lora, 2 * D), F32)
    seg = np.arange(SEG) // HEAD
    t = np.arange(WKV_CHUNKS * CHUNK)
    same_chunk = (t[None, :] // CHUNK) == (t[:, None] // CHUNK)
    return {
        "g_mix": row(g_mix), "w_in": w_in.astype(BF16), "b_gate": row(b_gate),
        "mu_prev": row(mu_prev), "mu_next": row(mu_next),
        "pool_w": pool_w.astype(BF16), "pool_scale": row(pool_scale), "w_pool_br": w_pool_br.astype(BF16),
        "k_k": row(k_k), "k_a": row(k_a), "r_k": row(r_k),
        "w0": 0.5 * jnp.concatenate([row(w0_f), row(w0_b)], axis=1),
        "a0": 0.5 * jnp.concatenate([row(a0_f), row(a0_b)], axis=1),
        "w_up": (0.5 * jnp.concatenate([jnp.concatenate([w_up_f, w_up_b], axis=1), zeros], axis=0)).astype(BF16),
        "a_up": (0.5 * jnp.concatenate([zeros, jnp.concatenate([a_up_f, a_up_b], axis=1)], axis=0)).astype(BF16),
        "g_up": g_up.astype(BF16), "ln_w": row(ln_w), "ln_b": row(ln_b),
        "w_rwkv_br": w_rwkv_br.astype(BF16), "w_out": w_out.astype(BF16),
        "g_ffn": row(g_ffn), "w_ff1": w_ff1.astype(BF16), "w_ff2": w_ff2.astype(BF16), "g_final": row(g_final),
        "bd": jnp.asarray(seg[:, None] == seg[None, :], BF16),
        "bd_mean": jnp.asarray((seg[:, None] == seg[None, :]) / HEAD, BF16),
        "tri_f": jnp.asarray(same_chunk & (t[None, :] <= t[:, None]), BF16),
        "tri_b": jnp.asarray(same_chunk & (t[None, :] >= t[:, None]), BF16),
    }


TM_PROJ = 1024
WKV_CHUNKS = 4
TM_OUT = 512


def _trunk(x, p):
    batch, seq, _ = x.shape
    xf = x.reshape(batch * seq, D)
    pooled, zs, gates = _in_proj(xf, seq, TM_PROJ, p)
    y_f, y_b, bonus = _wkv(zs, seq, WKV_CHUNKS, p)
    out = _out(y_f, y_b, bonus, zs, pooled, gates, xf, TM_OUT, p)
    return out.reshape(batch, seq, D)


def kernel(x_prompt, x_sample, g_mix, w_in, b_gate, mu_prev, mu_next, pool_w, pool_scale, w_pool_br, k_k, k_a, r_k, w0_f, w_up_f, a0_f, a_up_f, w0_b, w_up_b, a0_b, a_up_b, g_up, ln_w, ln_b, w_rwkv_br, w_out, g_ffn, w_ff1, w_ff2, g_final):
    depth = g_mix.shape[0]
    layers = [_prepare_params(g_mix[l], w_in[l], b_gate[l], mu_prev[l], mu_next[l], pool_w[l], pool_scale[l],
                              w_pool_br[l], k_k[l], k_a[l], r_k[l], w0_f[l], w_up_f[l], a0_f[l], a_up_f[l],
                              w0_b[l], w_up_b[l], a0_b[l], a_up_b[l], g_up[l], ln_w[l], ln_b[l], w_rwkv_br[l],
                              w_out[l], g_ffn[l], w_ff1[l], w_ff2[l], g_final) for l in range(depth)]
    assert depth == 1, "the final norm is fused into the last layer's ffn; only depth 1 is supported"
    return tuple(_trunk(x, layers[0]) for x in (x_prompt, x_sample))
```
